```python
import jax, jax.numpy as jnp
from jax import lax
import numpy as np

D_MODEL = 1024
BATCH = 2
SEQ = 8192
DEPTH = 4
DEC_BATCH = 128
DEC_SEQ = 1
PAST_LEN = 8192
PAGE_SIZE = 128

MIX_W = D_MODEL
FOX_HEADS = 4
FOX_DH = D_MODEL // 16
FOX_W = FOX_HEADS * FOX_DH
RG_W = MIX_W // 2
RG_HEADS = 8
RG_BLK = RG_W // RG_HEADS
RG_CONV = 4
RG_C = 8.0
MLA_HEADS = 4
MLA_NOPE = D_MODEL // 16
MLA_ROPE = D_MODEL // 32
MLA_DV = D_MODEL // 16
MLA_W = MLA_HEADS * MLA_DV
Q_LORA = 3 * D_MODEL // 16
KV_LORA = D_MODEL // 8
ROPE_BASE = 10000.0
D_FF = 11 * D_MODEL // 4
FFN_CONV = 3
Q_BLOCK = 128
EPS = 1e-6
IN_SIZES = (FOX_W, FOX_W, FOX_W, FOX_HEADS, RG_W, RG_W, Q_LORA, KV_LORA, MLA_ROPE)
N_IN = sum(IN_SIZES)

kernel_name = 'hybrid_fox_rglru_mla_convffn_step'


def rmsnorm(x, g):
    xf = x.astype(jnp.float32)
    y = xf * lax.rsqrt(jnp.mean(xf * xf, axis=-1, keepdims=True) + EPS)
    return (y * g).astype(x.dtype)


def rope(x, pos):
    half = x.shape[-1] // 2
    inv = ROPE_BASE ** (-jnp.arange(half, dtype=jnp.float32) / half)
    ang = pos.astype(jnp.float32)[:, None] * inv[None, :]
    cos = jnp.cos(ang)[None, :, None, :]
    sin = jnp.sin(ang)[None, :, None, :]
    x1 = x[..., :half].astype(jnp.float32)
    x2 = x[..., half:].astype(jnp.float32)
    return jnp.concatenate([x1 * cos - x2 * sin, x2 * cos + x1 * sin], axis=-1).astype(x.dtype)


def causal_dwconv(x, w, b, prev):
    K = w.shape[0]
    T = x.shape[1]
    xp = jnp.concatenate([prev.astype(x.dtype), x], axis=1)
    y = b + xp[:, 0:T] * w[0]
    for k in range(1, K):
        y = y + xp[:, k:k + T] * w[k]
    return y, xp[:, T:]


def _lin_combine(left, right):
    a1, b1 = left
    a2, b2 = right
    return a1 * a2, a2 * b1 + b2


def rglru(xc, wa, ba, wx, bx, lam, h0):
    B, T, W = xc.shape
    xb = xc.reshape(B, T, RG_HEADS, RG_BLK)
    r = jax.nn.sigmoid((jnp.einsum('bthi,hij->bthj', xb, wa).reshape(B, T, W) + ba).astype(jnp.float32))
    i = jax.nn.sigmoid((jnp.einsum('bthi,hij->bthj', xb, wx).reshape(B, T, W) + bx).astype(jnp.float32))
    log_a = -RG_C * r * jax.nn.softplus(-lam.astype(jnp.float32))
    a = jnp.exp(log_a)
    u = jnp.sqrt(-jnp.expm1(2.0 * log_a)) * (i * xc.astype(jnp.float32))
    u = u.at[:, 0].add(a[:, 0] * h0.astype(jnp.float32))
    _, hs = lax.associative_scan(_lin_combine, (a, u), axis=1)
    return hs, hs[:, -1]


def block_causal_attn(q, k, v, scale, f_q=None, f_k=None):
    B, T, H, _ = q.shape
    S = k.shape[1]
    shared = k.ndim == 3
    off = S - T
    qb = min(Q_BLOCK, T)
    nb = T // qb
    s_eq = 'bqhd,bkd->bhqk' if shared else 'bqhd,bkhd->bhqk'
    o_eq = 'bhqk,bkd->bqhd' if shared else 'bhqk,bkhd->bqhd'
    kpos = jnp.arange(S)
    fk = None if f_k is None else jnp.swapaxes(f_k, 1, 2)[:, :, None, :]

    def one_block(i):
        start = i * qb
        qs = lax.dynamic_slice_in_dim(q, start, qb, axis=1)
        s = jnp.einsum(s_eq, qs, k, preferred_element_type=jnp.float32) * scale
        if fk is not None:
            fq = lax.dynamic_slice_in_dim(f_q, start, qb, axis=1)
            s = s + jnp.swapaxes(fq, 1, 2)[..., None] - fk
        qpos = off + start + jnp.arange(qb)
        s = jnp.where(kpos[None, :] <= qpos[:, None], s, -jnp.inf)
        p = jax.nn.softmax(s, axis=-1).astype(v.dtype)
        return jnp.einsum(o_eq, p, v)

    o = lax.map(one_block, jnp.arange(nb))
    return jnp.moveaxis(o, 0, 1).reshape(B, T, H, v.shape[-1])


def token_mixers(h, pos, lw, past):
    B, T, _ = h.shape
    splits = np.cumsum(IN_SIZES)[:-1].tolist()
    z = h @ lw['w_in']
    zq, zk, zv, zf, zy, zx, zcq, zckv, zkr = jnp.split(z, splits, axis=-1)

    q = zq.reshape(B, T, FOX_HEADS, FOX_DH)
    k = zk.reshape(B, T, FOX_HEADS, FOX_DH)
    v = zv.reshape(B, T, FOX_HEADS, FOX_DH)
    logf = jax.nn.log_sigmoid((zf + lw['fox_bf']).astype(jnp.float32))
    if past is None:
        k_all, v_all, logf_all = k, v, logf
    else:
        k_all = jnp.concatenate([past['fox_k'].astype(k.dtype), k], axis=1)
        v_all = jnp.concatenate([past['fox_v'].astype(v.dtype), v], axis=1)
        logf_all = jnp.concatenate([past['fox_logf'].astype(jnp.float32), logf], axis=1)
    f_all = lax.cumsum(logf_all, axis=1)
    o_fox = block_causal_attn(q, k_all, v_all, FOX_DH ** -0.5, f_all[:, -T:], f_all)

    gate = jax.nn.gelu(zy)
    if past is None:
        conv_prev = jnp.zeros((B, RG_CONV - 1, RG_W), zx.dtype)
        h0 = jnp.zeros((B, RG_W), jnp.float32)
    else:
        conv_prev = past['rg_conv']
        h0 = past['rg_h']
    xc, rg_conv_new = causal_dwconv(zx, lw['rg_conv_w'], lw['rg_conv_b'], conv_prev)
    hs, h_last = rglru(xc, lw['rg_wa'], lw['rg_ba'], lw['rg_wx'], lw['rg_bx'], lw['rg_lam'], h0)
    o_rg = hs.astype(h.dtype) * gate

    cq = rmsnorm(zcq, lw['mla_q_g'])
    qh = jnp.einsum('btl,lhd->bthd', cq, lw['w_uq'])
    q_nope = qh[..., :MLA_NOPE]
    q_rope = rope(qh[..., MLA_NOPE:], pos)
    ckv = rmsnorm(zckv, lw['mla_kv_g'])
    kr = rope(zkr[:, :, None, :], pos)[:, :, 0]
    if past is None:
        ckv_all, kr_all = ckv, kr
    else:
        ckv_all = jnp.concatenate([past['mla_ckv'].astype(ckv.dtype), ckv], axis=1)
        kr_all = jnp.concatenate([past['mla_krope'].astype(kr.dtype), kr], axis=1)
    q_lat = jnp.einsum('bthn,lhn->bthl', q_nope, lw['w_uk'])
    q_cat = jnp.concatenate([q_lat, q_rope], axis=-1)
    k_cat = jnp.concatenate([ckv_all, kr_all], axis=-1)
    o_lat = block_causal_attn(q_cat, k_cat, ckv_all, (MLA_NOPE + MLA_ROPE) ** -0.5)
    o_mla = jnp.einsum('bthl,lhd->bthd', o_lat, lw['w_uv']).reshape(B, T, MLA_W)

    g = lw['mix_g']
    o = jnp.concatenate([
        rmsnorm(o_fox.reshape(B, T, FOX_W), g[:FOX_W]),
        rmsnorm(o_rg, g[FOX_W:FOX_W + RG_W]),
        rmsnorm(o_mla, g[FOX_W + RG_W:])], axis=-1)
    out = o @ lw['w_out']
    return out, (k, v, logf, ckv, kr, h_last, rg_conv_new)


def conv_ffn(h, lw, prev):
    up = h @ lw['w_up']
    upc, new_prev = causal_dwconv(up, lw['ffn_conv_w'], lw['ffn_conv_b'], prev)
    g, u = jnp.split(upc, 2, axis=-1)
    return (jax.nn.silu(g) * u) @ lw['w_down'], new_prev


def decoder_layer(x, c, pos, lw, past):
    B = x.shape[0]
    mod = jax.nn.silu(c) @ lw['w_ada'] + lw['b_ada']
    sh_a, sc_a, gt_a, sh_f, sc_f, gt_f = [m[:, None, :] for m in jnp.split(mod, 6, axis=-1)]
    h = rmsnorm(x, lw['norm1_g']) * (1.0 + sc_a) + sh_a
    mix, st = token_mixers(h, pos, lw, past)
    x = x + gt_a * mix
    h = rmsnorm(x, lw['norm2_g']) * (1.0 + sc_f) + sh_f
    prev = jnp.zeros((B, FFN_CONV - 1, 2 * D_FF), h.dtype) if past is None else past['ffn_conv']
    f, ffn_new = conv_ffn(h, lw, prev)
    x = x + gt_f * f
    return x, st + (ffn_new,)


def run_trunk(x, c, pos, W, final_g, past_fn):
    new = []
    for l in range(DEPTH):
        lw = {name: arr[l] for name, arr in W.items()}
        past = None if past_fn is None else past_fn(l)
        x, st = decoder_layer(x, c, pos, lw, past)
        new.append(st)
    y = rmsnorm(x, final_g)
    stacked = [jnp.stack([st[i] for st in new]) for i in range(len(new[0]))]
    return y, stacked


def setup_inputs(seed: int = 0) -> dict:
    key = jax.random.key(seed)
    ks = iter(jax.random.split(key, 48))

    def nrm(shape, scale=1.0):
        return scale * jax.random.normal(next(ks), shape, jnp.float32)

    n_pages = PAST_LEN // PAGE_SIZE
    n_pool = (DEC_BATCH * n_pages * 5) // 4
    page_table = jax.random.permutation(next(ks), n_pool)[:DEC_BATCH * n_pages].reshape(DEC_BATCH, n_pages).astype(jnp.int32)
    u = jax.random.uniform(next(ks), (DEPTH, RG_W), jnp.float32, 0.9, 0.999)
    s = u ** (1.0 / RG_C)
    rg_lam = jnp.log(s) - jnp.log1p(-s)
    d = D_MODEL
    inp = {}
    inp['x_prompt'] = nrm((BATCH, SEQ, d))
    inp['x_sample'] = nrm((DEC_BATCH, DEC_SEQ, d))
    inp['c_prompt'] = nrm((BATCH, d))
    inp['c_sample'] = nrm((DEC_BATCH, d))
    inp['cache_fox_k'] = nrm((DEPTH, n_pool, PAGE_SIZE, FOX_HEADS, FOX_DH))
    inp['cache_fox_v'] = nrm((DEPTH, n_pool, PAGE_SIZE, FOX_HEADS, FOX_DH))
    inp['cache_fox_logf'] = jax.nn.log_sigmoid(nrm((DEPTH, n_pool, PAGE_SIZE, FOX_HEADS)) + 2.0)
    inp['cache_mla_ckv'] = nrm((DEPTH, n_pool, PAGE_SIZE, KV_LORA))
    inp['cache_mla_krope'] = nrm((DEPTH, n_pool, PAGE_SIZE, MLA_ROPE))
    inp['state_rglru_h'] = nrm((DEPTH, DEC_BATCH, RG_W), 0.5)
    inp['state_rglru_conv'] = nrm((DEPTH, DEC_BATCH, RG_CONV - 1, RG_W))
    inp['state_ffn_conv'] = nrm((DEPTH, DEC_BATCH, FFN_CONV - 1, 2 * D_FF), 0.5)
    inp['page_table'] = page_table
    inp['norm1_g'] = 1.0 + nrm((DEPTH, d), 0.05)
    inp['norm2_g'] = 1.0 + nrm((DEPTH, d), 0.05)
    inp['w_ada'] = nrm((DEPTH, d, 6 * d), 0.5 * d ** -0.5)
    inp['b_ada'] = nrm((DEPTH, 6 * d), 0.02)
    inp['w_in'] = nrm((DEPTH, d, N_IN), d ** -0.5)
    inp['fox_bf'] = 2.0 + nrm((DEPTH, FOX_HEADS), 0.1)
    inp['rg_conv_w'] = nrm((DEPTH, RG_CONV, RG_W), RG_CONV ** -0.5)
    inp['rg_conv_b'] = nrm((DEPTH, RG_W), 0.02)
    inp['rg_wa'] = nrm((DEPTH, RG_HEADS, RG_BLK, RG_BLK), RG_BLK ** -0.5)
    inp['rg_ba'] = nrm((DEPTH, RG_W), 0.02)
    inp['rg_wx'] = nrm((DEPTH, RG_HEADS, RG_BLK, RG_BLK), RG_BLK ** -0.5)
    inp['rg_bx'] = nrm((DEPTH, RG_W), 0.02)
    inp['rg_lam'] = rg_lam
    inp['mla_q_g'] = 1.0 + nrm((DEPTH, Q_LORA), 0.05)
    inp['w_uq'] = nrm((DEPTH, Q_LORA, MLA_HEADS, MLA_NOPE + MLA_ROPE), Q_LORA ** -0.5)
    inp['mla_kv_g'] = 1.0 + nrm((DEPTH, KV_LORA), 0.05)
    inp['w_uk'] = nrm((DEPTH, KV_LORA, MLA_HEADS, MLA_NOPE), KV_LORA ** -0.5)
    inp['w_uv'] = nrm((DEPTH, KV_LORA, MLA_HEADS, MLA_DV), KV_LORA ** -0.5)
    inp['mix_g'] = 1.0 + nrm((DEPTH, MIX_W), 0.05)
    inp['w_out'] = nrm((DEPTH, MIX_W, d), MIX_W ** -0.5)
    inp['w_up'] = nrm((DEPTH, d, 2 * D_FF), d ** -0.5)
    inp['ffn_conv_w'] = nrm((DEPTH, FFN_CONV, 2 * D_FF), FFN_CONV ** -0.5)
    inp['ffn_conv_b'] = nrm((DEPTH, 2 * D_FF), 0.02)
    inp['w_down'] = nrm((DEPTH, D_FF, d), D_FF ** -0.5)
    inp['final_g'] = 1.0 + nrm((d,), 0.05)
    return inp


def reference(x_prompt, x_sample, c_prompt, c_sample, cache_fox_k, cache_fox_v, cache_fox_logf,
              cache_mla_ckv, cache_mla_krope, state_rglru_h, state_rglru_conv, state_ffn_conv, page_table,
              norm1_g, norm2_g, w_ada, b_ada, w_in, fox_bf, rg_conv_w, rg_conv_b, rg_wa, rg_ba, rg_wx, rg_bx,
              rg_lam, mla_q_g, w_uq, mla_kv_g, w_uk, w_uv, mix_g, w_out, w_up, ffn_conv_w, ffn_conv_b, w_down,
              final_g):
    W = {'norm1_g': norm1_g, 'norm2_g': norm2_g, 'w_ada': w_ada, 'b_ada': b_ada, 'w_in': w_in,
         'fox_bf': fox_bf, 'rg_conv_w': rg_conv_w, 'rg_conv_b': rg_conv_b, 'rg_wa': rg_wa, 'rg_ba': rg_ba,
         'rg_wx': rg_wx, 'rg_bx': rg_bx, 'rg_lam': rg_lam, 'mla_q_g': mla_q_g, 'w_uq': w_uq,
         'mla_kv_g': mla_kv_g, 'w_uk': w_uk, 'w_uv': w_uv, 'mix_g': mix_g, 'w_out': w_out, 'w_up': w_up,
         'ffn_conv_w': ffn_conv_w, 'ffn_conv_b': ffn_conv_b, 'w_down': w_down}
    page = cache_fox_k.shape[2]
    past_len = page_table.shape[1] * page
    dec_b = page_table.shape[0]
    pos_p = jnp.arange(x_prompt.shape[1], dtype=jnp.int32)
    pos_s = past_len + jnp.arange(x_sample.shape[1], dtype=jnp.int32)

    def gather(pool, l):
        g = pool[l, page_table]
        return g.reshape((dec_b, past_len) + g.shape[3:])

    def past_fn(l):
        return {'fox_k': gather(cache_fox_k, l), 'fox_v': gather(cache_fox_v, l),
                'fox_logf': gather(cache_fox_logf, l), 'mla_ckv': gather(cache_mla_ckv, l),
                'mla_krope': gather(cache_mla_krope, l), 'rg_h': state_rglru_h[l],
                'rg_conv': state_rglru_conv[l], 'ffn_conv': state_ffn_conv[l]}

    y_p, new_p = run_trunk(x_prompt, c_prompt, pos_p, W, final_g, None)
    y_s, new_s = run_trunk(x_sample, c_sample, pos_s, W, final_g, past_fn)
    fk_p, fv_p, lf_p, ckv_p, kr_p, rgh_p, rgc_p, ffc_p = new_p
    fk_s, fv_s, lf_s, ckv_s, kr_s, rgh_s, rgc_s, ffc_s = new_s
    return (y_p, y_s, fk_p, fk_s, fv_p, fv_s, lf_p, lf_s, ckv_p, ckv_s, kr_p, kr_s,
            rgh_p, rgh_s, rgc_p, rgc_s, ffc_p, ffc_s)
```

```python
import functools
import math

import numpy as np
import jax
import jax.numpy as jnp
from jax import lax
from jax.experimental import pallas as pl
from jax.experimental.pallas import tpu as pltpu

F32 = jnp.float32
BF16 = jnp.bfloat16

FOX_HEADS = 4
FOX_DH = 64
FOX_W = FOX_HEADS * FOX_DH
RG_W = 512
RG_HEADS = 8
RG_BLK = RG_W // RG_HEADS
RG_CONV = 4
RG_C = 8.0
MLA_HEADS = 4
MLA_NOPE = 64
MLA_ROPE = 32
MLA_DV = 64
MLA_W = MLA_HEADS * MLA_DV
Q_LORA = 192
KV_LORA = 128
ROPE_BASE = 10000.0
FFN_CONV = 3
EPS = 1e-6
MLA_SCALE = (MLA_NOPE + MLA_ROPE) ** -0.5
FOX_SCALE = FOX_DH ** -0.5

LANE = 128
SUB = 8
Q_LORA_PAD = 256

Z_Q, Z_K, Z_V, Z_Y, Z_X, Z_CQ, Z_CKV, Z_S = 0, 256, 512, 768, 1280, 1792, 2048, 2176
Z_N = 2304
S_F = 64
A_F = FOX_DH

VMEM_LIMIT = 56 * 1024 * 1024


def _cparams(sem):
    return pltpu.CompilerParams(dimension_semantics=sem, vmem_limit_bytes=VMEM_LIMIT)


def _dot(a, b):
    return jnp.dot(a, b, preferred_element_type=F32)


def _dot_nt(a, b):
    return lax.dot_general(a, b, (((1,), (1,)), ((), ())), preferred_element_type=F32)


def _rms(x, n=None):
    n = x.shape[-1] if n is None else n
    ms = jnp.sum(x * x, axis=-1, keepdims=True) * (1.0 / n)
    return x * lax.rsqrt(ms + EPS)


def _sigmoid(x):
    return 1.0 / (1.0 + jnp.exp(-x))


def _log_sigmoid(x):
    return jnp.minimum(x, 0.0) - jnp.log1p(jnp.exp(-jnp.abs(x)))


def _softplus(x):
    return jnp.maximum(x, 0.0) + jnp.log1p(jnp.exp(-jnp.abs(x)))


def _gelu_tanh(x):
    c = math.sqrt(2.0 / math.pi)
    return 0.5 * x * (1.0 + jnp.tanh(c * (x + 0.044715 * (x * x * x))))


def _silu(x):
    return x * _sigmoid(x)


def _cumsum_rows(x):
    n = x.shape[0]
    row = lax.broadcasted_iota(jnp.int32, x.shape, 0)
    s = 1
    while s < n:
        x = x + jnp.where(row >= s, pltpu.roll(x, s, 0), 0.0)
        s *= 2
    return x


def _lin_scan(a, u, h0):
    n, c = a.shape
    r8 = lax.broadcasted_iota(jnp.int32, a.shape, 0) & (SUB - 1)
    for s in (1, 2, 4):
        a_s = pltpu.roll(a, s, 0)
        u_s = pltpu.roll(u, s, 0)
        m = r8 >= s
        u = jnp.where(m, u + a * u_s, u)
        a = jnp.where(m, a * a_s, a)
    outs = []
    h = jnp.broadcast_to(h0, (SUB, c))
    for g in range(n // SUB):
        hb = a[g * SUB:(g + 1) * SUB] * h + u[g * SUB:(g + 1) * SUB]
        outs.append(hb)
        h = jnp.broadcast_to(hb[SUB - 1:SUB], (SUB, c))
    return jnp.concatenate(outs, axis=0)


def _split3(x):
    hi = x.astype(BF16).astype(F32)
    r = x - hi
    mid = r.astype(BF16).astype(F32)
    lo = (r - mid).astype(BF16).astype(F32)
    return hi, mid, lo


def _rope_small(zs, cs):
    t = zs * cs
    rot = t + pltpu.roll(t, LANE - MLA_ROPE, 1)
    lane = lax.broadcasted_iota(jnp.int32, zs.shape, 1)
    return jnp.where(lane < MLA_ROPE, rot, 0.0)


def _mla_queries(cq_in, gq, wuq, wuk, cs):
    cq = _rms(cq_in, Q_LORA) * gq
    qh = _dot(cq.astype(BF16), wuq)
    q_lat = _dot(qh[:, :MLA_HEADS * MLA_NOPE].astype(BF16), wuk)
    qrp = qh[:, MLA_HEADS * MLA_NOPE:]
    cs4 = jnp.concatenate([cs] * MLA_HEADS, axis=1)
    t = qrp * cs4
    w = t.shape[1]
    rot = t + pltpu.roll(t, w - MLA_ROPE, 1)
    lane = lax.broadcasted_iota(jnp.int32, t.shape, 1)
    q_rope = jnp.where((lane & (LANE - 1)) < MLA_ROPE, rot, 0.0)
    return q_lat * MLA_SCALE, q_rope * MLA_SCALE


def _rg_gates(xc, wg, bg, lam):
    gates = _dot(xc.astype(BF16), wg) + bg
    r = _sigmoid(gates[:, :RG_W])
    i = _sigmoid(gates[:, RG_W:])
    log_a = (-RG_C) * r * _softplus(-lam)
    a = jnp.exp(log_a)
    th = jnp.tanh(log_a)
    u = jnp.sqrt(-2.0 * th / (1.0 - th)) * (i * xc)
    return a, u


def _mod_body(c_ref, w_ref, b_ref, o_ref):
    c = c_ref[...]
    a = _silu(c).astype(BF16)
    o_ref[0] = _dot(a, w_ref[0].astype(BF16)) + b_ref[0]


def _mod_call(c_all, w_ada, b_ada):
    depth, d, n6 = w_ada.shape
    mp = c_all.shape[0]
    tn = 1024
    return pl.pallas_call(
        _mod_body,
        grid=(depth, n6 // tn),
        in_specs=[
            pl.BlockSpec((mp, d), lambda l, j: (0, 0)),
            pl.BlockSpec((1, d, tn), lambda l, j: (l, 0, j)),
            pl.BlockSpec((1, 1, tn), lambda l, j: (l, 0, j)),
        ],
        out_specs=pl.BlockSpec((1, mp, tn), lambda l, j: (l, 0, j)),
        out_shape=jax.ShapeDtypeStruct((depth, mp, n6), F32),
        compiler_params=_cparams(("arbitrary", "arbitrary")),
        name="adaln_mod",
    )(c_all, w_ada, b_ada.reshape(depth, 1, n6))


def _pin_body(x_ref, mod_ref, g1_ref, win_ref, bs_ref, cs_ref, pq_ref, pfq_ref, pk_ref, pfk_ref, cqk_ref,
              rgcw_ref, rgcb_ref, wg_ref, bg_ref, lam_ref, grg_ref, gq_ref, wuq_ref, wuk_ref, gkv_ref,
              fk_ref, fv_ref, lf_ref, qa_ref, ka_ref, vb_ref, ckv_ref, kr_ref, kcat_ref, qcat_ref,
              rgn_ref, rgh_ref, rgc_ref,
              fcar, hcar, xext, *, tm):
    t = pl.program_id(1)

    @pl.when(t == 0)
    def _():
        fcar[...] = jnp.zeros_like(fcar)
        hcar[...] = jnp.zeros_like(hcar)
        xext[0:SUB, :] = jnp.zeros((SUB, RG_W), F32)

    x = x_ref[0]
    mod = mod_ref[0, 0]
    h = _rms(x) * g1_ref[...] * (1.0 + mod[1:2, :]) + mod[0:1, :]
    hb = h.astype(BF16)

    zq = _dot(hb, win_ref[:, Z_Q:Z_Q + FOX_W])
    zk = _dot(hb, win_ref[:, Z_K:Z_K + FOX_W])
    zv = _dot(hb, win_ref[:, Z_V:Z_V + FOX_W])
    fk_ref[0] = zk
    fv_ref[0] = zv
    vb_ref[0] = zv.astype(BF16)
    zs = _dot(hb, win_ref[:, Z_S:Z_S + LANE])
    lane = lax.broadcasted_iota(jnp.int32, zs.shape, 1)
    fmask = (lane >= S_F) & (lane < S_F + FOX_HEADS)
    lf = jnp.where(fmask, _log_sigmoid(zs + bs_ref[...]), 0.0)
    lf_ref[0] = pltpu.roll(lf, LANE - S_F, 1)[:, :FOX_HEADS]
    fcum = _cumsum_rows(lf) + fcar[...]
    fcar[...] = fcum[tm - 1:tm, :]
    hi, mid, lo = _split3(fcum)
    fpack = (hi + pltpu.roll(mid, FOX_HEADS, 1) + pltpu.roll(lo, 2 * FOX_HEADS, 1)).astype(BF16)
    qa = _dot(zq.astype(BF16), pq_ref[...]) + _dot(fpack, pfq_ref[...]) + cqk_ref[0:1, :]
    ka = _dot(zk.astype(BF16), pk_ref[...]) + _dot(fpack, pfk_ref[...]) + cqk_ref[1:2, :]
    qa_ref[0] = qa.astype(BF16)
    ka_ref[0] = ka.astype(BF16)

    cs = cs_ref[...]
    q_lat, q_rope = _mla_queries(_dot(hb, win_ref[:, Z_CQ:Z_CQ + Q_LORA_PAD]), gq_ref[...], wuq_ref[...],
                                 wuk_ref[...], cs)
    for hh in range(MLA_HEADS):
        qcat_ref[0, hh, :, 0:LANE] = q_lat[:, hh * LANE:(hh + 1) * LANE].astype(BF16)
        qcat_ref[0, hh, :, LANE:2 * LANE] = q_rope[:, hh * LANE:(hh + 1) * LANE].astype(BF16)
    ckv = _rms(_dot(hb, win_ref[:, Z_CKV:Z_CKV + KV_LORA])) * gkv_ref[...]
    ckv_ref[0] = ckv
    kr = _rope_small(zs, cs)
    kr_ref[0] = kr[:, :MLA_ROPE]
    kcat_ref[0, :, 0:LANE] = ckv.astype(BF16)
    kcat_ref[0, :, LANE:2 * LANE] = kr.astype(BF16)

    zy = _dot(hb, win_ref[:, Z_Y:Z_Y + RG_W])
    zx = _dot(hb, win_ref[:, Z_X:Z_X + RG_W])
    xext[SUB:SUB + tm, :] = zx
    cw = rgcw_ref[...]
    xc = rgcb_ref[...] + xext[SUB - 3:SUB - 3 + tm, :] * cw[0:1, :]
    xc = xc + xext[SUB - 2:SUB - 2 + tm, :] * cw[1:2, :]
    xc = xc + xext[SUB - 1:SUB - 1 + tm, :] * cw[2:3, :]
    xc = xc + zx * cw[3:4, :]
    last = xext[tm:tm + SUB, :]
    rgc_ref[0] = last
    xext[0:SUB, :] = last
    a, u = _rg_gates(xc, wg_ref[...], bg_ref[...], lam_ref[...])
    hs = _lin_scan(a, u, hcar[...])
    hcar[...] = hs[tm - 1:tm, :]
    rgh_ref[0] = hs[tm - SUB:tm, :]
    o = hs * _gelu_tanh(zy)
    rgn_ref[0] = (_rms(o) * grg_ref[...]).astype(BF16)


def _pin_call(x, mod, lw, consts, cs_tab, *, tm):
    b, t, d = x.shape
    nt = t // tm
    full = lambda shape: pl.BlockSpec(shape, lambda i, j: (0,) * len(shape))
    in_specs = [
        pl.BlockSpec((1, tm, d), lambda i, j: (i, j, 0)),
        pl.BlockSpec((1, 1, 6, d), lambda i, j: (i, 0, 0, 0)),
        full((1, d)), full((d, Z_N)), full((1, LANE)),
        pl.BlockSpec((tm, LANE), lambda i, j: (j, 0)),
        full((FOX_W, 4 * LANE)), full((LANE, 4 * LANE)), full((FOX_W, 4 * LANE)), full((LANE, 4 * LANE)),
        full((2, 4 * LANE)),
        full((RG_CONV, RG_W)), full((1, RG_W)), full((RG_W, 2 * RG_W)), full((1, 2 * RG_W)), full((1, RG_W)),
        full((1, RG_W)),
        full((1, Q_LORA_PAD)), full((Q_LORA_PAD, 768)), full((256, 512)), full((1, KV_LORA)),
    ]
    row = lambda w: pl.BlockSpec((1, tm, w), lambda i, j: (i, j, 0))
    last8 = lambda w: pl.BlockSpec((1, SUB, w), lambda i, j: (i, 0, 0))
    out_specs = [
        row(FOX_W), row(FOX_W), row(FOX_HEADS), row(4 * LANE), row(4 * LANE), row(FOX_W),
        row(KV_LORA), row(MLA_ROPE), row(2 * LANE),
        pl.BlockSpec((1, MLA_HEADS, tm, 2 * LANE), lambda i, j: (i, 0, j, 0)),
        row(RG_W), last8(RG_W), last8(RG_W),
    ]
    sds = lambda shape, dt: jax.ShapeDtypeStruct(shape, dt)
    out_shape = [
        sds((b, t, FOX_W), F32), sds((b, t, FOX_W), F32), sds((b, t, FOX_HEADS), F32),
        sds((b, t, 4 * LANE), BF16), sds((b, t, 4 * LANE), BF16), sds((b, t, FOX_W), BF16),
        sds((b, t, KV_LORA), F32), sds((b, t, MLA_ROPE), F32), sds((b, t, 2 * LANE), BF16),
        sds((b, MLA_HEADS, t, 2 * LANE), BF16),
        sds((b, t, RG_W), BF16), sds((b, SUB, RG_W), F32), sds((b, SUB, RG_W), F32),
    ]
    return pl.pallas_call(
        functools.partial(_pin_body, tm=tm),
        grid=(b, nt),
        in_specs=in_specs,
        out_specs=out_specs,
        out_shape=out_shape,
        scratch_shapes=[pltpu.VMEM((1, LANE), F32), pltpu.VMEM((1, RG_W), F32), pltpu.VMEM((tm + SUB, RG_W), F32)],
        compiler_params=_cparams(("arbitrary", "arbitrary")),
        name="prompt_in",
    )(x, mod, lw["g1"], lw["win"], lw["bs"], cs_tab, consts["pq"], consts["pfq"], consts["pk"], consts["pfk"],
      consts["cqk"], lw["rgcw"], lw["rgcb"], lw["wg"], lw["bg"], lw["lam"], lw["g_rg"], lw["gq"], lw["wuq"],
      lw["wuk"], lw["gkv"])


def _pfox_body(qi_ref, ki_ref, qa_ref, ka_ref, v_ref, g_ref, o_ref, m_sc, l_sc, acc_sc, *, tq):
    p = pl.program_id(1)
    qi = qi_ref[p]
    ki = ki_ref[p]

    @pl.when(ki == 0)
    def _():
        m_sc[...] = jnp.full_like(m_sc, -jnp.inf)
        l_sc[...] = jnp.zeros_like(l_sc)
        acc_sc[...] = jnp.zeros_like(acc_sc)

    def step(masked):
        v = v_ref[0]
        for hh in range(FOX_HEADS):
            q = qa_ref[0, :, hh * LANE:(hh + 1) * LANE]
            k = ka_ref[0, :, hh * LANE:(hh + 1) * LANE]
            s = _dot_nt(q, k)
            if masked:
                r = lax.broadcasted_iota(jnp.int32, s.shape, 0)
                c = lax.broadcasted_iota(jnp.int32, s.shape, 1)
                s = jnp.where(c <= r, s, -jnp.inf)
            m_prev = m_sc[hh]
            m_new = jnp.maximum(m_prev, jnp.max(s, axis=1, keepdims=True))
            alpha = jnp.exp(m_prev - m_new)
            pr = jnp.exp(s - m_new)
            l_sc[hh] = alpha * l_sc[hh] + jnp.sum(pr, axis=1, keepdims=True)
            acc_sc[hh] = alpha * acc_sc[hh] + _dot(pr.astype(BF16), v)
            m_sc[hh] = m_new

    @pl.when(ki < qi)
    def _():
        step(False)

    @pl.when(ki == qi)
    def _():
        step(True)
        lane = lax.broadcasted_iota(jnp.int32, (tq, FOX_W), 1)
        o = jnp.zeros((tq, FOX_W), F32)
        for hh in range(FOX_HEADS):
            sel = (lane >= hh * FOX_DH) & (lane < (hh + 1) * FOX_DH)
            o = jnp.where(sel, acc_sc[hh] * (1.0 / l_sc[hh]), o)
        o_ref[0] = (_rms(o) * g_ref[...]).astype(BF16)


def _tri_tables(n):
    qi = np.concatenate([np.full(i + 1, i, np.int32) for i in range(n)])
    ki = np.concatenate([np.arange(i + 1, dtype=np.int32) for i in range(n)])
    return jnp.asarray(qi), jnp.asarray(ki)


def _pfox_call(qa, ka, vb, g_fox, *, tq):
    b, t, _ = qa.shape
    nq = t // tq
    qi_tab, ki_tab = _tri_tables(nq)
    grid_spec = pltpu.PrefetchScalarGridSpec(
        num_scalar_prefetch=2,
        grid=(b, int(qi_tab.shape[0])),
        in_specs=[
            pl.BlockSpec((1, tq, 4 * LANE), lambda i, p, qi, ki: (i, qi[p], 0)),
            pl.BlockSpec((1, tq, 4 * LANE), lambda i, p, qi, ki: (i, ki[p], 0)),
            pl.BlockSpec((1, tq, FOX_W), lambda i, p, qi, ki: (i, ki[p], 0)),
            pl.BlockSpec((1, FOX_W), lambda i, p, qi, ki: (0, 0)),
        ],
        out_specs=pl.BlockSpec((1, tq, FOX_W), lambda i, p, qi, ki: (i, qi[p], 0)),
        scratch_shapes=[pltpu.VMEM((FOX_HEADS, tq, 1), F32), pltpu.VMEM((FOX_HEADS, tq, 1), F32),
                        pltpu.VMEM((FOX_HEADS, tq, FOX_W), F32)],
    )
    return pl.pallas_call(
        functools.partial(_pfox_body, tq=tq),
        grid_spec=grid_spec,
        out_shape=jax.ShapeDtypeStruct((b, t, FOX_W), BF16),
        compiler_params=_cparams(("arbitrary", "arbitrary")),
        name="prompt_fox_attn",
    )(qi_tab, ki_tab, qa, ka, vb, g_fox)


def _pmla_body(qi_ref, ki_ref, q_ref, k_ref, wuv_ref, g_ref, o_ref, m_sc, l_sc, acc_sc, *, tq):
    p = pl.program_id(1)
    qi = qi_ref[p]
    ki = ki_ref[p]
    rows = MLA_HEADS * tq

    @pl.when(ki == 0)
    def _():
        m_sc[...] = jnp.full_like(m_sc, -jnp.inf)
        l_sc[...] = jnp.zeros_like(l_sc)
        acc_sc[...] = jnp.zeros_like(acc_sc)

    def step(masked):
        q = q_ref[0].reshape(rows, 2 * LANE)
        k = k_ref[0]
        s = _dot_nt(q, k)
        if masked:
            r = lax.broadcasted_iota(jnp.int32, s.shape, 0) & (tq - 1)
            c = lax.broadcasted_iota(jnp.int32, s.shape, 1)
            s = jnp.where(c <= r, s, -jnp.inf)
        m_prev = m_sc[...]
        m_new = jnp.maximum(m_prev, jnp.max(s, axis=1, keepdims=True))
        alpha = jnp.exp(m_prev - m_new)
        pr = jnp.exp(s - m_new)
        l_sc[...] = alpha * l_sc[...] + jnp.sum(pr, axis=1, keepdims=True)
        acc_sc[...] = alpha * acc_sc[...] + _dot(pr.astype(BF16), k)
        m_sc[...] = m_new

    @pl.when(ki < qi)
    def _():
        step(False)

    @pl.when(ki == qi)
    def _():
        step(True)
        o = jnp.zeros((tq, MLA_W), F32)
        for hh in range(MLA_HEADS):
            sl = slice(hh * tq, (hh + 1) * tq)
            o_lat = acc_sc[sl, 0:KV_LORA] * (1.0 / l_sc[sl, :])
            o = o + _dot(o_lat.astype(BF16), wuv_ref[hh])
        o_ref[0] = (_rms(o) * g_ref[...]).astype(BF16)


def _pmla_call(qcat, kcat, wuv, g_mla, *, tq):
    b, _, t, _ = qcat.shape
    nq = t // tq
    qi_tab, ki_tab = _tri_tables(nq)
    grid_spec = pltpu.PrefetchScalarGridSpec(
        num_scalar_prefetch=2,
        grid=(b, int(qi_tab.shape[0])),
        in_specs=[
            pl.BlockSpec((1, MLA_HEADS, tq, 2 * LANE), lambda i, p, qi, ki: (i, 0, qi[p], 0)),
            pl.BlockSpec((1, tq, 2 * LANE), lambda i, p, qi, ki: (i, ki[p], 0)),
            pl.BlockSpec((MLA_HEADS, KV_LORA, MLA_W), lambda i, p, qi, ki: (0, 0, 0)),
            pl.BlockSpec((1, MLA_W), lambda i, p, qi, ki: (0, 0)),
        ],
        out_specs=pl.BlockSpec((1, tq, MLA_W), lambda i, p, qi, ki: (i, qi[p], 0)),
        scratch_shapes=[pltpu.VMEM((MLA_HEADS * tq, 1), F32), pltpu.VMEM((MLA_HEADS * tq, 1), F32),
                        pltpu.VMEM((MLA_HEADS * tq, 2 * LANE), F32)],
    )
    return pl.pallas_call(
        functools.partial(_pmla_body, tq=tq),
        grid_spec=grid_spec,
        out_shape=jax.ShapeDtypeStruct((b, t, MLA_W), BF16),
        compiler_params=_cparams(("arbitrary", "arbitrary")),
        name="prompt_mla_attn",
    )(qi_tab, ki_tab, qcat, kcat, wuv, g_mla)


def _ffn_conv_act(g, u, ext, pg, pu, cwg, cwu, cbg, cbu, n):
    def conv(cur, prev, cw, cb):
        ext[0:SUB, :] = prev
        ext[SUB:SUB + n, :] = cur
        y = cb + ext[SUB - 2:SUB - 2 + n, :] * cw[0:1, :]
        y = y + ext[SUB - 1:SUB - 1 + n, :] * cw[1:2, :]
        return y + cur * cw[2:3, :]
    gc = conv(g, pg, cwg, cbg)
    uc = conv(u, pu, cwu, cbu)
    return (_silu(gc) * uc).astype(BF16)


def _pffn_body(x_ref, fox_ref, rg_ref, mla_ref, wout_ref, mod_ref, g2_ref, wg_ref, wu_ref, cwg_ref, cwu_ref,
               cbg_ref, cbu_ref, wd_ref, gfin_ref,
               o_ref, stg_ref, stu_ref,
               x1_sc, h2_sc, acc_sc, ext_sc, carg_sc, caru_sc, *, tm, nc, final):
    t = pl.program_id(1)
    c = pl.program_id(2)
    mod = mod_ref[0, 0]

    @pl.when(c == 0)
    def _():
        mix = _dot(fox_ref[0], wout_ref[0:FOX_W, :])
        mix = mix + _dot(rg_ref[0], wout_ref[FOX_W:FOX_W + RG_W, :])
        mix = mix + _dot(mla_ref[0], wout_ref[FOX_W + RG_W:, :])
        x1 = x_ref[0] + mod[2:3, :] * mix
        x1_sc[...] = x1
        h2_sc[...] = (_rms(x1) * g2_ref[...] * (1.0 + mod[4:5, :]) + mod[3:4, :]).astype(BF16)
        acc_sc[...] = jnp.zeros_like(acc_sc)

    @pl.when(t == 0)
    def _():
        carg_sc[c] = jnp.zeros(carg_sc.shape[1:], F32)
        caru_sc[c] = jnp.zeros(caru_sc.shape[1:], F32)

    h2 = h2_sc[...]
    g = _dot(h2, wg_ref[...])
    u = _dot(h2, wu_ref[...])
    act = _ffn_conv_act(g, u, ext_sc, carg_sc[c], caru_sc[c], cwg_ref[...], cwu_ref[...], cbg_ref[...],
                        cbu_ref[...], tm)
    g_last = g[tm - SUB:tm, :]
    u_last = u[tm - SUB:tm, :]
    carg_sc[c] = g_last
    caru_sc[c] = u_last
    stg_ref[0, 0] = g_last
    stu_ref[0, 0] = u_last
    acc_sc[...] += _dot(act, wd_ref[...])

    @pl.when(c == nc - 1)
    def _():
        out = x1_sc[...] + mod[5:6, :] * acc_sc[...]
        if final:
            out = _rms(out) * gfin_ref[...]
        o_ref[0] = out


def _pffn_call(x, fox_n, rg_n, mla_n, mod, lw, gfin, *, tm, final):
    b, t, d = x.shape
    dff = lw["wd"].shape[0]
    nc = 2
    fc = dff // nc
    nt = t // tm
    full = lambda shape: pl.BlockSpec(shape, lambda i, j, c: (0,) * len(shape))
    row = lambda w: pl.BlockSpec((1, tm, w), lambda i, j, c: (i, j, 0))
    in_specs = [
        row(d), row(FOX_W), row(RG_W), row(MLA_W), full((d, d)),
        pl.BlockSpec((1, 1, 6, d), lambda i, j, c: (i, 0, 0, 0)), full((1, d)),
        pl.BlockSpec((d, fc), lambda i, j, c: (0, c)),
        pl.BlockSpec((d, fc), lambda i, j, c: (0, nc + c)),
        pl.BlockSpec((FFN_CONV, fc), lambda i, j, c: (0, c)),
        pl.BlockSpec((FFN_CONV, fc), lambda i, j, c: (0, nc + c)),
        pl.BlockSpec((1, fc), lambda i, j, c: (0, c)),
        pl.BlockSpec((1, fc), lambda i, j, c: (0, nc + c)),
        pl.BlockSpec((fc, d), lambda i, j, c: (c, 0)),
        full((1, d)),
    ]
    out_specs = [
        row(d),
        pl.BlockSpec((1, 1, SUB, fc), lambda i, j, c: (i, j, 0, c)),
        pl.BlockSpec((1, 1, SUB, fc), lambda i, j, c: (i, j, 0, c)),
    ]
    out_shape = [jax.ShapeDtypeStruct((b, t, d), F32), jax.ShapeDtypeStruct((b, nt, SUB, dff), F32),
                 jax.ShapeDtypeStruct((b, nt, SUB, dff), F32)]
    return pl.pallas_call(
        functools.partial(_pffn_body, tm=tm, nc=nc, final=final),
        grid=(b, nt, nc),
        in_specs=in_specs,
        out_specs=out_specs,
        out_shape=out_shape,
        scratch_shapes=[pltpu.VMEM((tm, d), F32), pltpu.VMEM((tm, d), BF16), pltpu.VMEM((tm, d), F32),
                        pltpu.VMEM((tm + SUB, fc), F32), pltpu.VMEM((nc, SUB, fc), F32),
                        pltpu.VMEM((nc, SUB, fc), F32)],
        compiler_params=_cparams(("arbitrary", "arbitrary", "arbitrary")),
        name="prompt_out_ffn",
    )(x, fox_n, rg_n, mla_n, lw["wout"], mod, lw["g2"], lw["wup"], lw["wup"], lw["fcw"], lw["fcw"],
      lw["fcb"], lw["fcb"], lw["wd"], gfin)


def _sin_body(x_ref, mod_ref, g1_ref, win_ref, bs_ref, cs_ref, rgcw_ref, rgcb_ref, wg_ref, bg_ref, lam_ref,
              grg_ref, gq_ref, wuq_ref, wuk_ref, gkv_ref, h0_ref, cprev_ref,
              fq_ref, fk_ref, fv_ref, lfl_ref, lf_ref, qlat_ref, qtail_ref, ckv_ref, tail_ref, kr_ref,
              rgn_ref, rgh_ref, zx_ref):
    d = x_ref.shape[1]
    x = x_ref[...]
    sh_a = mod_ref[:, 0:d]
    sc_a = mod_ref[:, d:2 * d]
    h = _rms(x) * g1_ref[...] * (1.0 + sc_a) + sh_a
    hb = h.astype(BF16)

    fq_ref[...] = _dot(hb, win_ref[:, Z_Q:Z_Q + FOX_W]) * FOX_SCALE
    fk_ref[...] = _dot(hb, win_ref[:, Z_K:Z_K + FOX_W])
    fv_ref[...] = _dot(hb, win_ref[:, Z_V:Z_V + FOX_W])
    zs = _dot(hb, win_ref[:, Z_S:Z_S + LANE])
    lane = lax.broadcasted_iota(jnp.int32, zs.shape, 1)
    fmask = (lane >= S_F) & (lane < S_F + FOX_HEADS)
    lf = jnp.where(fmask, _log_sigmoid(zs + bs_ref[...]), 0.0)
    lfl_ref[...] = lf
    lf_ref[...] = pltpu.roll(lf, LANE - S_F, 1)[:, :FOX_HEADS]

    cs = jnp.broadcast_to(cs_ref[...], zs.shape)
    q_lat, q_rope = _mla_queries(_dot(hb, win_ref[:, Z_CQ:Z_CQ + Q_LORA_PAD]), gq_ref[...], wuq_ref[...],
                                 wuk_ref[...], cs)
    qlat_ref[...] = q_lat
    qtail_ref[...] = q_rope
    ckv = _rms(_dot(hb, win_ref[:, Z_CKV:Z_CKV + KV_LORA])) * gkv_ref[...]
    ckv_ref[...] = ckv
    kr = _rope_small(zs, cs)
    tail_ref[...] = kr
    kr_ref[...] = kr[:, :MLA_ROPE]

    zy = _dot(hb, win_ref[:, Z_Y:Z_Y + RG_W])
    zx = _dot(hb, win_ref[:, Z_X:Z_X + RG_W])
    zx_ref[...] = zx
    cw = rgcw_ref[...]
    xc = rgcb_ref[...] + cprev_ref[0] * cw[0:1, :]
    xc = xc + cprev_ref[1] * cw[1:2, :]
    xc = xc + cprev_ref[2] * cw[2:3, :]
    xc = xc + zx * cw[3:4, :]
    a, u = _rg_gates(xc, wg_ref[...], bg_ref[...], lam_ref[...])
    hn = a * h0_ref[...] + u
    rgh_ref[...] = hn
    o = hn * _gelu_tanh(zy)
    rgn_ref[...] = (_rms(o) * grg_ref[...]).astype(BF16)


def _sin_call(x, mod, lw, cs_row, h0, cprev):
    n, d = x.shape
    sds = lambda w, dt=F32: jax.ShapeDtypeStruct((n, w), dt)
    out_shape = [sds(FOX_W), sds(FOX_W), sds(FOX_W), sds(LANE), sds(FOX_HEADS), sds(4 * LANE), sds(4 * LANE),
                 sds(KV_LORA), sds(LANE), sds(MLA_ROPE), sds(RG_W, BF16), sds(RG_W), sds(RG_W)]
    return pl.pallas_call(
        _sin_body,
        out_shape=out_shape,
        compiler_params=pltpu.CompilerParams(vmem_limit_bytes=VMEM_LIMIT),
        name="sample_in",
    )(x, mod, lw["g1"], lw["win"], lw["bs"], cs_row, lw["rgcw"], lw["rgcb"], lw["wg"], lw["bg"], lw["lam"],
      lw["g_rg"], lw["gq"], lw["wuq"], lw["wuk"], lw["gkv"], h0, cprev)


def _rows_from_lanes(row_vec, width, n_heads):
    r = lax.broadcasted_iota(jnp.int32, (SUB, width), 0)
    out = jnp.zeros((SUB, width), F32)
    for hh in range(n_heads):
        piece = jnp.broadcast_to(row_vec[:, hh * width:(hh + 1) * width], (SUB, width))
        out = jnp.where(r == hh, piece, out)
    return out


def _sattn_body(pt_ref, fq_ref, fkn_ref, fvn_ref, lfn_ref, qlat_ref, qtail_ref, ckvn_ref, tailn_ref,
                kc_ref, vc_ref, lc_ref, cc_ref, rc_ref,
                ofox_ref, olat_ref,
                kbuf, vbuf, lbuf, cbuf, rbuf, sems, mf_sc, lf_sc, af_sc, mm_sc, lm_sc, am_sc, tail_sc,
                *, pc, nch, nb, layer):
    b = pl.program_id(0)
    c = pl.program_id(1)
    step = b * nch + c
    slot = step % 2
    total = nb * nch

    def copies(bb, cc, sl, j):
        pg = pt_ref[bb, (nch - 1 - cc) * pc + j]
        return (
            pltpu.make_async_copy(kc_ref.at[layer, pg], kbuf.at[sl, j], sems.at[0, sl]),
            pltpu.make_async_copy(vc_ref.at[layer, pg], vbuf.at[sl, j], sems.at[1, sl]),
            pltpu.make_async_copy(lc_ref.at[layer, pg], lbuf.at[sl, j], sems.at[2, sl]),
            pltpu.make_async_copy(cc_ref.at[layer, pg], cbuf.at[sl, j], sems.at[3, sl]),
            pltpu.make_async_copy(rc_ref.at[layer, pg], rbuf.at[sl, j], sems.at[4, sl]),
        )

    def start_chunk(bb, cc, sl):
        def body(j, carry):
            for cp in copies(bb, cc, sl, j):
                cp.start()
            return carry
        lax.fori_loop(0, pc, body, 0)

    def wait_chunk(bb, cc, sl):
        def body(j, carry):
            for cp in copies(bb, cc, sl, j):
                cp.wait()
            return carry
        lax.fori_loop(0, pc, body, 0)

    @pl.when(step == 0)
    def _():
        start_chunk(b, c, slot)

    @pl.when(step + 1 < total)
    def _():
        nxt = step + 1
        start_chunk(nxt // nch, nxt % nch, 1 - slot)

    lane256 = lax.broadcasted_iota(jnp.int32, (SUB, FOX_W), 1)
    row256 = lax.broadcasted_iota(jnp.int32, (SUB, FOX_W), 0)
    headsel = (lane256 >= row256 * FOX_DH) & (lane256 < (row256 + 1) * FOX_DH)
    qf = jnp.where(headsel, jnp.broadcast_to(fq_ref[0], (SUB, FOX_W)), 0.0)
    ql = _rows_from_lanes(qlat_ref[0], LANE, MLA_HEADS)
    qt = _rows_from_lanes(qtail_ref[0], LANE, MLA_HEADS)[:, :MLA_ROPE]
    lane128 = lax.broadcasted_iota(jnp.int32, (SUB, LANE), 1)
    row128 = lax.broadcasted_iota(jnp.int32, (SUB, LANE), 0)
    lfn = jnp.broadcast_to(lfn_ref[0], (SUB, LANE))
    lf_new = jnp.sum(jnp.where(lane128 == row128 + S_F, lfn, 0.0), axis=1, keepdims=True)

    @pl.when(c == 0)
    def _():
        mf_sc[...] = jnp.full_like(mf_sc, -jnp.inf)
        lf_sc[...] = jnp.zeros_like(lf_sc)
        af_sc[...] = jnp.zeros_like(af_sc)
        mm_sc[...] = jnp.full_like(mm_sc, -jnp.inf)
        lm_sc[...] = jnp.zeros_like(lm_sc)
        am_sc[...] = jnp.zeros_like(am_sc)
        tail_sc[...] = lf_new

    wait_chunk(b, c, slot)

    qf_b = qf.astype(BF16)
    ql_b = ql.astype(BF16)
    qt_b = qt.astype(BF16)

    s_f, s_m, tot = [], [], []
    zero4 = jnp.zeros((SUB - FOX_HEADS, LANE), F32)
    for j in range(pc):
        s_f.append(_dot(qf_b, kbuf[slot, j].astype(BF16)))
        sm = _dot_nt(ql_b, cbuf[slot, j].astype(BF16)) + _dot(qt_b, rbuf[slot, j].astype(BF16))
        s_m.append(sm)
        x = jnp.concatenate([lbuf[slot, j], zero4], axis=0)
        incl = x
        sh = 1
        while sh < LANE:
            incl = incl + jnp.where(lane128 + sh < LANE, pltpu.roll(incl, LANE - sh, 1), 0.0)
            sh *= 2
        s_f[j] = s_f[j] + (incl - x)
        tot.append(incl[:, 0:1])
    run = tail_sc[...]
    for j in range(pc - 1, -1, -1):
        s_f[j] = s_f[j] + run
        run = run + tot[j]
    tail_sc[...] = run

    def online(s_list, m_sc, l_sc):
        m_prev = m_sc[...]
        mx = s_list[0]
        for s in s_list[1:]:
            mx = jnp.maximum(mx, s)
        m_new = jnp.maximum(m_prev, jnp.max(mx, axis=1, keepdims=True))
        alpha = jnp.exp(m_prev - m_new)
        ps = [jnp.exp(s - m_new) for s in s_list]
        sm = ps[0]
        for p_ in ps[1:]:
            sm = sm + p_
        l_sc[...] = alpha * l_sc[...] + jnp.sum(sm, axis=1, keepdims=True)
        m_sc[...] = m_new
        return alpha, ps

    alpha_f, p_f = online(s_f, mf_sc, lf_sc)
    acc = jnp.zeros((SUB, FOX_W), F32)
    for j in range(pc):
        acc = acc + _dot_nt(p_f[j].astype(BF16), vbuf[slot, j].astype(BF16))
    af_sc[...] = alpha_f * af_sc[...] + acc

    alpha_m, p_m = online(s_m, mm_sc, lm_sc)
    accm = jnp.zeros((SUB, KV_LORA), F32)
    for j in range(pc):
        accm = accm + _dot(p_m[j].astype(BF16), cbuf[slot, j].astype(BF16))
    am_sc[...] = alpha_m * am_sc[...] + accm

    @pl.when(c == nch - 1)
    def _():
        kn = jnp.broadcast_to(fkn_ref[0], (SUB, FOX_W))
        vn = jnp.broadcast_to(fvn_ref[0], (SUB, FOX_W))
        s_new = jnp.sum(qf * kn, axis=1, keepdims=True)
        m_prev = mf_sc[...]
        m_new = jnp.maximum(m_prev, s_new)
        al = jnp.exp(m_prev - m_new)
        pn = jnp.exp(s_new - m_new)
        l_fin = al * lf_sc[...] + pn
        o8 = (al * af_sc[...] + pn * vn) * (1.0 / l_fin)
        ofox_ref[0] = jnp.sum(jnp.where(headsel, o8, 0.0), axis=0, keepdims=True)

        cn = jnp.broadcast_to(ckvn_ref[0], (SUB, KV_LORA))
        tn = jnp.broadcast_to(tailn_ref[0], (SUB, LANE))
        qt_full = _rows_from_lanes(qtail_ref[0], LANE, MLA_HEADS)
        s_new = jnp.sum(ql * cn, axis=1, keepdims=True) + jnp.sum(qt_full * tn, axis=1, keepdims=True)
        m_prev = mm_sc[...]
        m_new = jnp.maximum(m_prev, s_new)
        al = jnp.exp(m_prev - m_new)
        pn = jnp.exp(s_new - m_new)
        l_fin = al * lm_sc[...] + pn
        olat_ref[0] = (al * am_sc[...] + pn * cn) * (1.0 / l_fin)


def _sattn_call(page_table, fq, fkn, fvn, lfn, qlat, qtail, ckvn, tailn, kc, vc, lc, cc, rc, *, nch, layer):
    nb, npg = page_table.shape
    pc = npg // nch
    page = kc.shape[-1]
    r3 = lambda a: a.reshape(nb, 1, a.shape[-1])
    vec = lambda w: pl.BlockSpec((1, 1, w), lambda i, j, pt: (i, 0, 0))
    any_spec = pl.BlockSpec(memory_space=pl.ANY)
    grid_spec = pltpu.PrefetchScalarGridSpec(
        num_scalar_prefetch=1,
        grid=(nb, nch),
        in_specs=[vec(FOX_W), vec(FOX_W), vec(FOX_W), vec(LANE), vec(4 * LANE), vec(4 * LANE), vec(KV_LORA),
                  vec(LANE), any_spec, any_spec, any_spec, any_spec, any_spec],
        out_specs=[pl.BlockSpec((1, 1, FOX_W), lambda i, j, pt: (i, 0, 0)),
                   pl.BlockSpec((1, SUB, KV_LORA), lambda i, j, pt: (i, 0, 0))],
        scratch_shapes=[
            pltpu.VMEM((2, pc, FOX_W, page), F32), pltpu.VMEM((2, pc, FOX_W, page), F32),
            pltpu.VMEM((2, pc, FOX_HEADS, page), F32), pltpu.VMEM((2, pc, page, KV_LORA), F32),
            pltpu.VMEM((2, pc, MLA_ROPE, page), F32), pltpu.SemaphoreType.DMA((5, 2)),
            pltpu.VMEM((SUB, 1), F32), pltpu.VMEM((SUB, 1), F32), pltpu.VMEM((SUB, FOX_W), F32),
            pltpu.VMEM((SUB, 1), F32), pltpu.VMEM((SUB, 1), F32), pltpu.VMEM((SUB, KV_LORA), F32),
            pltpu.VMEM((SUB, 1), F32),
        ],
    )
    return pl.pallas_call(
        functools.partial(_sattn_body, pc=pc, nch=nch, nb=nb, layer=layer),
        grid_spec=grid_spec,
        out_shape=[jax.ShapeDtypeStruct((nb, 1, FOX_W), F32), jax.ShapeDtypeStruct((nb, SUB, KV_LORA), F32)],
        compiler_params=_cparams(("arbitrary", "arbitrary")),
        name="sample_paged_attn",
    )(page_table, r3(fq), r3(fkn), r3(fvn), r3(lfn), r3(qlat), r3(qtail), r3(ckvn), r3(tailn), kc, vc, lc, cc, rc)


def _sffn_body(x_ref, ofox_ref, rg_ref, olat_ref, wuv_ref, gfox_ref, gmla_ref, wout_ref, mod_ref, g2_ref,
               wg_ref, wu_ref, cwg_ref, cwu_ref, cbg_ref, cbu_ref, p0g_ref, p0u_ref, p1g_ref, p1u_ref, wd_ref,
               gfin_ref,
               o_ref, upg_ref, upu_ref,
               x1_sc, h2_sc, acc_sc, *, nc, final):
    c = pl.program_id(0)
    d = x_ref.shape[1]

    @pl.when(c == 0)
    def _():
        fox_n = (_rms(ofox_ref[...]) * gfox_ref[...]).astype(BF16)
        o_mla = _dot(olat_ref[:, 0:MLA_HEADS * KV_LORA].astype(BF16), wuv_ref[...])
        mla_n = (_rms(o_mla) * gmla_ref[...]).astype(BF16)
        mix = _dot(fox_n, wout_ref[0:FOX_W, :])
        mix = mix + _dot(rg_ref[...], wout_ref[FOX_W:FOX_W + RG_W, :])
        mix = mix + _dot(mla_n, wout_ref[FOX_W + RG_W:, :])
        x1 = x_ref[...] + mod_ref[:, 2 * d:3 * d] * mix
        x1_sc[...] = x1
        h2_sc[...] = (_rms(x1) * g2_ref[...] * (1.0 + mod_ref[:, 4 * d:5 * d]) + mod_ref[:, 3 * d:4 * d]).astype(BF16)
        acc_sc[...] = jnp.zeros_like(acc_sc)

    h2 = h2_sc[...]
    g = _dot(h2, wg_ref[...])
    u = _dot(h2, wu_ref[...])
    upg_ref[...] = g
    upu_ref[...] = u
    cwg = cwg_ref[...]
    cwu = cwu_ref[...]
    gc = cbg_ref[...] + p0g_ref[...] * cwg[0:1, :] + p1g_ref[...] * cwg[1:2, :] + g * cwg[2:3, :]
    uc = cbu_ref[...] + p0u_ref[...] * cwu[0:1, :] + p1u_ref[...] * cwu[1:2, :] + u * cwu[2:3, :]
    act = (_silu(gc) * uc).astype(BF16)
    acc_sc[...] += _dot(act, wd_ref[...])

    @pl.when(c == nc - 1)
    def _():
        out = x1_sc[...] + mod_ref[:, 5 * d:6 * d] * acc_sc[...]
        if final:
            out = _rms(out) * gfin_ref[...]
        o_ref[...] = out


def _sffn_call(x, ofox, rg_n, olat, mod, lw, prev, gfin, *, final):
    n, d = x.shape
    dff = lw["wd"].shape[0]
    nc = 2
    fc = dff // nc
    full = lambda shape: pl.BlockSpec(shape, lambda c: (0,) * len(shape))
    col = lambda rows, off: pl.BlockSpec((rows, fc), lambda c, off=off: (0, off + c))
    in_specs = [
        full((n, d)), full((n, FOX_W)), full((n, RG_W)), full((n, SUB * KV_LORA)),
        full((MLA_HEADS * KV_LORA, MLA_W)), full((1, FOX_W)), full((1, MLA_W)), full((d, d)), full((n, 6 * d)),
        full((1, d)),
        col(d, 0), col(d, nc), col(FFN_CONV, 0), col(FFN_CONV, nc), col(1, 0), col(1, nc),
        col(n, 0), col(n, nc), col(n, 2 * nc), col(n, 3 * nc),
        pl.BlockSpec((fc, d), lambda c: (c, 0)),
        full((1, d)),
    ]
    out_specs = [full((n, d)), col(n, 0), col(n, 0)]
    out_shape = [jax.ShapeDtypeStruct((n, d), F32), jax.ShapeDtypeStruct((n, dff), F32),
                 jax.ShapeDtypeStruct((n, dff), F32)]
    return pl.pallas_call(
        functools.partial(_sffn_body, nc=nc, final=final),
        grid=(nc,),
        in_specs=in_specs,
        out_specs=out_specs,
        out_shape=out_shape,
        scratch_shapes=[pltpu.VMEM((n, d), F32), pltpu.VMEM((n, d), BF16), pltpu.VMEM((n, d), F32)],
        compiler_params=_cparams(("arbitrary",)),
        name="sample_out_ffn",
    )(x, ofox, rg_n, olat, lw["wuv_bd"], lw["g_fox"], lw["g_mla"], lw["wout"], mod, lw["g2"], lw["wup"], lw["wup"],
      lw["fcw"], lw["fcw"], lw["fcb"], lw["fcb"], prev, prev, prev, prev, lw["wd"], gfin)


def _swap_half(w):
    half = w.shape[-1] // 2
    return jnp.concatenate([-w[..., half:], w[..., :half]], axis=-1)


def _prep_weights(norm1_g, norm2_g, w_in, fox_bf, rg_conv_w, rg_conv_b, rg_wa, rg_ba, rg_wx, rg_bx, rg_lam,
                  mla_q_g, w_uq, mla_kv_g, w_uk, w_uv, mix_g, w_out, w_up, ffn_conv_w, ffn_conv_b, w_down):
    depth, d, _ = w_in.shape
    o = np.cumsum((0, FOX_W, FOX_W, FOX_W, FOX_HEADS, RG_W, RG_W, Q_LORA, KV_LORA, MLA_ROPE))
    seg = lambda i: w_in[:, :, o[i]:o[i + 1]]
    wq, wk, wv, wf, wy, wx, wcq, wckv, wkr = [seg(i) for i in range(9)]
    z = lambda n: jnp.zeros((depth, d, n), w_in.dtype)
    win = jnp.concatenate([wq, wk, wv, wy, wx, wcq, z(Q_LORA_PAD - Q_LORA), wckv, wkr, _swap_half(wkr), wf,
                           z(LANE - 2 * MLA_ROPE - FOX_HEADS)], axis=-1).astype(BF16)
    bs = jnp.zeros((depth, 1, LANE), F32).at[:, 0, S_F:S_F + FOX_HEADS].set(fox_bf)

    eye_rg = jnp.eye(RG_HEADS, dtype=F32)
    bd = lambda w: jnp.einsum("lhij,hg->lhigj", w, eye_rg).reshape(depth, RG_W, RG_W)
    wg = jnp.concatenate([bd(rg_wa), bd(rg_wx)], axis=-1).astype(BF16)
    bg = jnp.concatenate([rg_ba, rg_bx], axis=-1).reshape(depth, 1, 2 * RG_W)

    nope = w_uq[..., :MLA_NOPE].reshape(depth, Q_LORA, MLA_HEADS * MLA_NOPE)
    rope = w_uq[..., MLA_NOPE:]
    rope_pad = jnp.concatenate([rope, _swap_half(rope),
                                jnp.zeros((depth, Q_LORA, MLA_HEADS, LANE - 2 * MLA_ROPE), F32)], axis=-1)
    wuq = jnp.concatenate([nope, rope_pad.reshape(depth, Q_LORA, MLA_HEADS * LANE)], axis=-1)
    wuq = jnp.pad(wuq, ((0, 0), (0, Q_LORA_PAD - Q_LORA), (0, 0))).astype(BF16)
    eye_h = jnp.eye(MLA_HEADS, dtype=F32)
    wuk = jnp.einsum("lchn,hg->lhngc", w_uk, eye_h).reshape(depth, MLA_HEADS * MLA_NOPE, MLA_HEADS * KV_LORA)
    wuv = jnp.einsum("lchd,hg->lhcgd", w_uv, eye_h).reshape(depth, MLA_HEADS, KV_LORA, MLA_W)
    gq = jnp.pad(mla_q_g, ((0, 0), (0, Q_LORA_PAD - Q_LORA))).reshape(depth, 1, Q_LORA_PAD)

    r2 = lambda a: a.reshape(depth, 1, a.shape[-1])
    return dict(
        g1=r2(norm1_g), g2=r2(norm2_g), win=win, bs=bs,
        rgcw=rg_conv_w, rgcb=r2(rg_conv_b), wg=wg, bg=bg, lam=r2(rg_lam),
        gq=gq, wuq=wuq, wuk=wuk.astype(BF16), gkv=r2(mla_kv_g),
        wuv=wuv.astype(BF16), wuv_bd=wuv.reshape(depth, MLA_HEADS * KV_LORA, MLA_W).astype(BF16),
        g_fox=r2(mix_g[:, :FOX_W]), g_rg=r2(mix_g[:, FOX_W:FOX_W + RG_W]), g_mla=r2(mix_g[:, FOX_W + RG_W:]),
        wout=w_out.astype(BF16), wup=w_up.astype(BF16), fcw=ffn_conv_w, fcb=r2(ffn_conv_b), wd=w_down.astype(BF16),
    )


def _fox_placement():
    pq = np.zeros((FOX_W, 4 * LANE), np.float32)
    pk = np.zeros((FOX_W, 4 * LANE), np.float32)
    pfq = np.zeros((LANE, 4 * LANE), np.float32)
    pfk = np.zeros((LANE, 4 * LANE), np.float32)
    cqk = np.zeros((2, 4 * LANE), np.float32)
    for h in range(FOX_HEADS):
        for dd in range(FOX_DH):
            pq[h * FOX_DH + dd, h * LANE + dd] = FOX_SCALE
            pk[h * FOX_DH + dd, h * LANE + dd] = 1.0
        for piece in range(3):
            src = S_F + piece * FOX_HEADS + h
            pfq[src, h * LANE + A_F + piece] = 1.0
            cqk[0, h * LANE + A_F + 3 + piece] = 1.0
            cqk[1, h * LANE + A_F + piece] = 1.0
            pfk[src, h * LANE + A_F + 3 + piece] = -1.0
    return dict(pq=jnp.asarray(pq, BF16), pk=jnp.asarray(pk, BF16), pfq=jnp.asarray(pfq, BF16),
                pfk=jnp.asarray(pfk, BF16), cqk=jnp.asarray(cqk, F32))


def _rope_table(pos):
    half = MLA_ROPE // 2
    inv = ROPE_BASE ** (-jnp.arange(half, dtype=F32) / half)
    ang = pos.astype(F32)[:, None] * inv[None, :]
    cos, sin = jnp.cos(ang), jnp.sin(ang)
    return jnp.concatenate([cos, cos, sin, sin, jnp.zeros((pos.shape[0], LANE - 2 * MLA_ROPE), F32)], axis=-1)


def _forward(x_prompt, x_sample, c_prompt, c_sample, cache_fox_k, cache_fox_v, cache_fox_logf, cache_mla_ckv,
             cache_mla_krope, state_rglru_h, state_rglru_conv, state_ffn_conv, page_table, norm1_g, norm2_g, w_ada,
             b_ada, w_in, fox_bf, rg_conv_w, rg_conv_b, rg_wa, rg_ba, rg_wx, rg_bx, rg_lam, mla_q_g, w_uq, mla_kv_g,
             w_uk, w_uv, mix_g, w_out, w_up, ffn_conv_w, ffn_conv_b, w_down, final_g, *, tm, tq, nch):
    depth = w_in.shape[0]
    bp, t, d = x_prompt.shape
    bs = x_sample.shape[0]
    npool, page = cache_fox_k.shape[1], cache_fox_k.shape[2]
    past_len = page_table.shape[1] * page
    dff = w_down.shape[1]

    wts = _prep_weights(norm1_g, norm2_g, w_in, fox_bf, rg_conv_w, rg_conv_b, rg_wa, rg_ba, rg_wx, rg_bx, rg_lam,
                        mla_q_g, w_uq, mla_kv_g, w_uk, w_uv, mix_g, w_out, w_up, ffn_conv_w, ffn_conv_b, w_down)
    consts = _fox_placement()
    gfin = final_g.reshape(1, d)

    mp = -(-(bp + bs) // SUB) * SUB
    c_all = jnp.concatenate([c_prompt, c_sample, jnp.zeros((mp - bp - bs, d), F32)], axis=0)
    mod_all = _mod_call(c_all, w_ada, b_ada)
    mod_p = mod_all[:, :bp].reshape(depth, bp, 1, 6, d)
    mod_s = mod_all[:, bp:bp + bs]

    cs_p = _rope_table(jnp.arange(t, dtype=jnp.int32))
    cs_s = _rope_table(jnp.full((1,), past_len, jnp.int32))

    kc = jnp.transpose(cache_fox_k, (0, 1, 3, 4, 2)).reshape(depth, npool, FOX_W, page)
    vc = jnp.transpose(cache_fox_v, (0, 1, 3, 4, 2)).reshape(depth, npool, FOX_W, page)
    lc = jnp.transpose(cache_fox_logf, (0, 1, 3, 2))
    rc = jnp.transpose(cache_mla_krope, (0, 1, 3, 2))
    conv_prev = jnp.transpose(state_rglru_conv, (0, 2, 1, 3))
    ffn_prev = state_ffn_conv.reshape(depth, bs, (FFN_CONV - 1) * 2 * dff)

    outs_p = [[] for _ in range(8)]
    outs_s = [[] for _ in range(8)]
    xp = x_prompt
    xs = x_sample.reshape(bs, d)
    for l in range(depth):
        lw = {k: v[l] for k, v in wts.items()}
        final = l == depth - 1

        (fk, fv, lf, qa, ka, vb, ckv, kr, kcat, qcat, rgn, rgh8, rgc8) = _pin_call(xp, mod_p[l], lw, consts, cs_p, tm=tm)
        fox_n = _pfox_call(qa, ka, vb, lw["g_fox"], tq=tq)
        mla_n = _pmla_call(qcat, kcat, lw["wuv"], lw["g_mla"], tq=tq)
        xp, stg, stu = _pffn_call(xp, fox_n, rgn, mla_n, mod_p[l], lw, gfin, tm=tm, final=final)
        ffc = jnp.concatenate([stg[:, -1], stu[:, -1]], axis=-1)[:, SUB - (FFN_CONV - 1):]
        for lst, val in zip(outs_p, (fk.reshape(bp, t, FOX_HEADS, FOX_DH), fv.reshape(bp, t, FOX_HEADS, FOX_DH), lf,
                                     ckv, kr, rgh8[:, SUB - 1], rgc8[:, SUB - (RG_CONV - 1):], ffc)):
            lst.append(val)

        (fq, fkn, fvn, lfl, lfn, qlat, qtail, ckvn, tailn, krn, rgn_s, rgh_s, zx_s) = _sin_call(
            xs, mod_s[l], lw, cs_s, state_rglru_h[l], conv_prev[l])
        ofox, olat = _sattn_call(page_table, fq, fkn, fvn, lfl, qlat, qtail, ckvn, tailn,
                                 kc, vc, lc, cache_mla_ckv, rc, nch=nch, layer=l)
        xs, upg, upu = _sffn_call(xs, ofox.reshape(bs, FOX_W), rgn_s, olat.reshape(bs, SUB * KV_LORA), mod_s[l], lw,
                                  ffn_prev[l], gfin, final=final)
        rgc_new = jnp.concatenate([state_rglru_conv[l][:, 1:], zx_s[:, None, :]], axis=1)
        ffc_new = jnp.stack([state_ffn_conv[l][:, 1], jnp.concatenate([upg, upu], axis=-1)], axis=1)
        for lst, val in zip(outs_s, (fkn.reshape(bs, 1, FOX_HEADS, FOX_DH), fvn.reshape(bs, 1, FOX_HEADS, FOX_DH),
                                     lfn.reshape(bs, 1, FOX_HEADS), ckvn.reshape(bs, 1, KV_LORA),
                                     krn.reshape(bs, 1, MLA_ROPE), rgh_s, rgc_new, ffc_new)):
            lst.append(val)

    sp = [jnp.stack(v) for v in outs_p]
    ss = [jnp.stack(v) for v in outs_s]
    res = [xp, xs.reshape(bs, 1, d)]
    for a, b_ in zip(sp, ss):
        res += [a, b_]
    return tuple(res)


def kernel(x_prompt, x_sample, c_prompt, c_sample, cache_fox_k, cache_fox_v, cache_fox_logf, cache_mla_ckv, cache_mla_krope, state_rglru_h, state_rglru_conv, state_ffn_conv, page_table, norm1_g, norm2_g, w_ada, b_ada, w_in, fox_bf, rg_conv_w, rg_conv_b, rg_wa, rg_ba, rg_wx, rg_bx, rg_lam, mla_q_g, w_uq, mla_kv_g, w_uk, w_uv, mix_g, w_out, w_up, ffn_conv_w, ffn_conv_b, w_down, final_g):
    t = x_prompt.shape[1]
    npg = page_table.shape[1]
    tm = min(512, t)
    tq = min(512, t)
    nch = 2 if npg % 2 == 0 and npg >= 2 else 1
    return _forward(x_prompt, x_sample, c_prompt, c_sample, cache_fox_k, cache_fox_v, cache_fox_logf, cache_mla_ckv,
                    cache_mla_krope, state_rglru_h, state_rglru_conv, state_ffn_conv, page_table, norm1_g, norm2_g,
                    w_ada, b_ada, w_in, fox_bf, rg_conv_w, rg_conv_b, rg_wa, rg_ba, rg_wx, rg_bx, rg_lam, mla_q_g,
                    w_uq, mla_kv_g, w_uk, w_uv, mix_g, w_out, w_up, ffn_conv_w, ffn_conv_b, w_down, final_g,
                    tm=tm, tq=tq, nch=nch)
```

```python
import functools
import math

import numpy as np
import jax
import jax.numpy as jnp
from jax import lax
from jax.experimental import pallas as pl
from jax.experimental.pallas import tpu as pltpu

F32 = jnp.float32
BF16 = jnp.bfloat16

FOX_HEADS = 4
FOX_DH = 64
FOX_W = FOX_HEADS * FOX_DH
RG_W = 512
RG_HEADS = 8
RG_BLK = RG_W // RG_HEADS
RG_CONV = 4
RG_C = 8.0
MLA_HEADS = 4
MLA_NOPE = 64
MLA_ROPE = 32
MLA_DV = 64
MLA_W = MLA_HEADS * MLA_DV
Q_LORA = 192
KV_LORA = 128
ROPE_BASE = 10000.0
FFN_CONV = 3
EPS = 1e-6
MLA_SCALE = (MLA_NOPE + MLA_ROPE) ** -0.5
FOX_SCALE = FOX_DH ** -0.5

LANE = 128
SUB = 8
Q_LORA_PAD = 256

Z_Q, Z_K, Z_V, Z_Y, Z_X, Z_CQ, Z_CKV, Z_S = 0, 256, 512, 768, 1280, 1792, 2048, 2176
Z_N = 2304
S_F = 64
A_F = FOX_DH

VMEM_LIMIT = 56 * 1024 * 1024


def _cparams(sem):
    return pltpu.CompilerParams(dimension_semantics=sem, vmem_limit_bytes=VMEM_LIMIT)


def _dot(a, b):
    return jnp.dot(a, b, preferred_element_type=F32)


def _dot_nt(a, b):
    return lax.dot_general(a, b, (((1,), (1,)), ((), ())), preferred_element_type=F32)


def _rms(x, n=None):
    n = x.shape[-1] if n is None else n
    ms = jnp.sum(x * x, axis=-1, keepdims=True) * (1.0 / n)
    return x * lax.rsqrt(ms + EPS)


def _sigmoid(x):
    return 1.0 / (1.0 + jnp.exp(-x))


def _log_sigmoid(x):
    return jnp.minimum(x, 0.0) - jnp.log1p(jnp.exp(-jnp.abs(x)))


def _softplus(x):
    return jnp.maximum(x, 0.0) + jnp.log1p(jnp.exp(-jnp.abs(x)))


def _gelu_tanh(x):
    c = math.sqrt(2.0 / math.pi)
    return 0.5 * x * (1.0 + jnp.tanh(c * (x + 0.044715 * (x * x * x))))


def _silu(x):
    return x * _sigmoid(x)


def _cumsum_rows(x):
    n = x.shape[0]
    row = lax.broadcasted_iota(jnp.int32, x.shape, 0)
    s = 1
    while s < n:
        x = x + jnp.where(row >= s, pltpu.roll(x, s, 0), 0.0)
        s *= 2
    return x


def _lin_scan(a, u, h0):
    n, c = a.shape
    r8 = lax.broadcasted_iota(jnp.int32, a.shape, 0) & (SUB - 1)
    for s in (1, 2, 4):
        a_s = pltpu.roll(a, s, 0)
        u_s = pltpu.roll(u, s, 0)
        m = r8 >= s
        u = jnp.where(m, u + a * u_s, u)
        a = jnp.where(m, a * a_s, a)
    outs = []
    h = jnp.broadcast_to(h0, (SUB, c))
    for g in range(n // SUB):
        hb = a[g * SUB:(g + 1) * SUB] * h + u[g * SUB:(g + 1) * SUB]
        outs.append(hb)
        h = jnp.broadcast_to(hb[SUB - 1:SUB], (SUB, c))
    return jnp.concatenate(outs, axis=0)


def _split3(x):
    hi = x.astype(BF16).astype(F32)
    r = x - hi
    mid = r.astype(BF16).astype(F32)
    lo = (r - mid).astype(BF16).astype(F32)
    return hi, mid, lo


def _rope_small(zs, cs):
    t = zs * cs
    rot = t + pltpu.roll(t, LANE - MLA_ROPE, 1)
    lane = lax.broadcasted_iota(jnp.int32, zs.shape, 1)
    return jnp.where(lane < MLA_ROPE, rot, 0.0)


def _mla_queries(cq_in, gq, wuq, wuk, cs):
    cq = _rms(cq_in, Q_LORA) * gq
    qh = _dot(cq.astype(BF16), wuq)
    q_lat = _dot(qh[:, :MLA_HEADS * MLA_NOPE].astype(BF16), wuk)
    qrp = qh[:, MLA_HEADS * MLA_NOPE:]
    cs4 = jnp.concatenate([cs] * MLA_HEADS, axis=1)
    t = qrp * cs4
    w = t.shape[1]
    rot = t + pltpu.roll(t, w - MLA_ROPE, 1)
    lane = lax.broadcasted_iota(jnp.int32, t.shape, 1)
    q_rope = jnp.where((lane & (LANE - 1)) < MLA_ROPE, rot, 0.0)
    return q_lat * MLA_SCALE, q_rope * MLA_SCALE


def _rg_gates(xc, wg, bg, lam):
    gates = _dot(xc.astype(BF16), wg) + bg
    r = _sigmoid(gates[:, :RG_W])
    i = _sigmoid(gates[:, RG_W:])
    log_a = (-RG_C) * r * _softplus(-lam)
    a = jnp.exp(log_a)
    th = jnp.tanh(log_a)
    u = jnp.sqrt(-2.0 * th / (1.0 - th)) * (i * xc)
    return a, u


def _mod_body(c_ref, w_ref, b_ref, o_ref):
    c = c_ref[...]
    a = _silu(c).astype(BF16)
    o_ref[0] = _dot(a, w_ref[0].astype(BF16)) + b_ref[0]


def _mod_call(c_all, w_ada, b_ada):
    depth, d, n6 = w_ada.shape
    mp = c_all.shape[0]
    tn = 1024
    return pl.pallas_call(
        _mod_body,
        grid=(depth, n6 // tn),
        in_specs=[
            pl.BlockSpec((mp, d), lambda l, j: (0, 0)),
            pl.BlockSpec((1, d, tn), lambda l, j: (l, 0, j)),
            pl.BlockSpec((1, 1, tn), lambda l, j: (l, 0, j)),
        ],
        out_specs=pl.BlockSpec((1, mp, tn), lambda l, j: (l, 0, j)),
        out_shape=jax.ShapeDtypeStruct((depth, mp, n6), F32),
        compiler_params=_cparams(("arbitrary", "arbitrary")),
        name="adaln_mod",
    )(c_all, w_ada, b_ada.reshape(depth, 1, n6))


def _pin_body(x_ref, mod_ref, g1_ref, win_ref, bs_ref, cs_ref, pq_ref, pfq_ref, pk_ref, pfk_ref, cqk_ref,
              rgcw_ref, rgcb_ref, wg_ref, bg_ref, lam_ref, grg_ref, gq_ref, wuq_ref, wuk_ref, gkv_ref,
              fk_ref, fv_ref, lf_ref, qa_ref, ka_ref, vb_ref, ckv_ref, kr_ref, kcat_ref, qcat_ref,
              rgn_ref, rgh_ref, rgc_ref,
              fcar, hcar, xext, *, tm):
    t = pl.program_id(1)

    @pl.when(t == 0)
    def _():
        fcar[...] = jnp.zeros_like(fcar)
        hcar[...] = jnp.zeros_like(hcar)
        xext[0:SUB, :] = jnp.zeros((SUB, RG_W), F32)

    x = x_ref[0]
    mod = mod_ref[0, 0]
    h = _rms(x) * g1_ref[...] * (1.0 + mod[1:2, :]) + mod[0:1, :]
    hb = h.astype(BF16)

    zq = _dot(hb, win_ref[:, Z_Q:Z_Q + FOX_W])
    zk = _dot(hb, win_ref[:, Z_K:Z_K + FOX_W])
    zv = _dot(hb, win_ref[:, Z_V:Z_V + FOX_W])
    fk_ref[0] = zk
    fv_ref[0] = zv
    vb_ref[0] = zv.astype(BF16)
    zs = _dot(hb, win_ref[:, Z_S:Z_S + LANE])
    lane = lax.broadcasted_iota(jnp.int32, zs.shape, 1)
    fmask = (lane >= S_F) & (lane < S_F + FOX_HEADS)
    lf = jnp.where(fmask, _log_sigmoid(zs + bs_ref[...]), 0.0)
    lf_ref[0] = pltpu.roll(lf, LANE - S_F, 1)[:, :FOX_HEADS]
    fcum = _cumsum_rows(lf) + fcar[...]
    fcar[...] = fcum[tm - 1:tm, :]
    hi, mid, lo = _split3(fcum)
    fpack = (hi + pltpu.roll(mid, FOX_HEADS, 1) + pltpu.roll(lo, 2 * FOX_HEADS, 1)).astype(BF16)
    qa = _dot(zq.astype(BF16), pq_ref[...]) + _dot(fpack, pfq_ref[...]) + cqk_ref[0:1, :]
    ka = _dot(zk.astype(BF16), pk_ref[...]) + _dot(fpack, pfk_ref[...]) + cqk_ref[1:2, :]
    qa_ref[0] = qa.astype(BF16)
    ka_ref[0] = ka.astype(BF16)

    cs = cs_ref[...]
    q_lat, q_rope = _mla_queries(_dot(hb, win_ref[:, Z_CQ:Z_CQ + Q_LORA_PAD]), gq_ref[...], wuq_ref[...],
                                 wuk_ref[...], cs)
    for hh in range(MLA_HEADS):
        qcat_ref[0, hh, :, 0:LANE] = q_lat[:, hh * LANE:(hh + 1) * LANE].astype(BF16)
        qcat_ref[0, hh, :, LANE:2 * LANE] = q_rope[:, hh * LANE:(hh + 1) * LANE].astype(BF16)
    ckv = _rms(_dot(hb, win_ref[:, Z_CKV:Z_CKV + KV_LORA])) * gkv_ref[...]
    ckv_ref[0] = ckv
    kr = _rope_small(zs, cs)
    kr_ref[0] = kr[:, :MLA_ROPE]
    kcat_ref[0, :, 0:LANE] = ckv.astype(BF16)
    kcat_ref[0, :, LANE:2 * LANE] = kr.astype(BF16)

    zy = _dot(hb, win_ref[:, Z_Y:Z_Y + RG_W])
    zx = _dot(hb, win_ref[:, Z_X:Z_X + RG_W])
    xext[SUB:SUB + tm, :] = zx
    cw = rgcw_ref[...]
    xc = rgcb_ref[...] + xext[SUB - 3:SUB - 3 + tm, :] * cw[0:1, :]
    xc = xc + xext[SUB - 2:SUB - 2 + tm, :] * cw[1:2, :]
    xc = xc + xext[SUB - 1:SUB - 1 + tm, :] * cw[2:3, :]
    xc = xc + zx * cw[3:4, :]
    last = xext[tm:tm + SUB, :]
    rgc_ref[0] = last
    xext[0:SUB, :] = last
    a, u = _rg_gates(xc, wg_ref[...], bg_ref[...], lam_ref[...])
    hs = _lin_scan(a, u, hcar[...])
    hcar[...] = hs[tm - 1:tm, :]
    rgh_ref[0] = hs[tm - SUB:tm, :]
    o = hs * _gelu_tanh(zy)
    rgn_ref[0] = (_rms(o) * grg_ref[...]).astype(BF16)


def _pin_call(x, mod, lw, consts, cs_tab, *, tm):
    b, t, d = x.shape
    nt = t // tm
    full = lambda shape: pl.BlockSpec(shape, lambda i, j: (0,) * len(shape))
    in_specs = [
        pl.BlockSpec((1, tm, d), lambda i, j: (i, j, 0)),
        pl.BlockSpec((1, 1, 6, d), lambda i, j: (i, 0, 0, 0)),
        full((1, d)), full((d, Z_N)), full((1, LANE)),
        pl.BlockSpec((tm, LANE), lambda i, j: (j, 0)),
        full((FOX_W, 4 * LANE)), full((LANE, 4 * LANE)), full((FOX_W, 4 * LANE)), full((LANE, 4 * LANE)),
        full((2, 4 * LANE)),
        full((RG_CONV, RG_W)), full((1, RG_W)), full((RG_W, 2 * RG_W)), full((1, 2 * RG_W)), full((1, RG_W)),
        full((1, RG_W)),
        full((1, Q_LORA_PAD)), full((Q_LORA_PAD, 768)), full((256, 512)), full((1, KV_LORA)),
    ]
    row = lambda w: pl.BlockSpec((1, tm, w), lambda i, j: (i, j, 0))
    last8 = lambda w: pl.BlockSpec((1, SUB, w), lambda i, j: (i, 0, 0))
    out_specs = [
        row(FOX_W), row(FOX_W), row(FOX_HEADS), row(4 * LANE), row(4 * LANE), row(FOX_W),
        row(KV_LORA), row(MLA_ROPE), row(2 * LANE),
        pl.BlockSpec((1, MLA_HEADS, tm, 2 * LANE), lambda i, j: (i, 0, j, 0)),
        row(RG_W), last8(RG_W), last8(RG_W),
    ]
    sds = lambda shape, dt: jax.ShapeDtypeStruct(shape, dt)
    out_shape = [
        sds((b, t, FOX_W), F32), sds((b, t, FOX_W), F32), sds((b, t, FOX_HEADS), F32),
        sds((b, t, 4 * LANE), BF16), sds((b, t, 4 * LANE), BF16), sds((b, t, FOX_W), BF16),
        sds((b, t, KV_LORA), F32), sds((b, t, MLA_ROPE), F32), sds((b, t, 2 * LANE), BF16),
        sds((b, MLA_HEADS, t, 2 * LANE), BF16),
        sds((b, t, RG_W), BF16), sds((b, SUB, RG_W), F32), sds((b, SUB, RG_W), F32),
    ]
    return pl.pallas_call(
        functools.partial(_pin_body, tm=tm),
        grid=(b, nt),
        in_specs=in_specs,
        out_specs=out_specs,
        out_shape=out_shape,
        scratch_shapes=[pltpu.VMEM((1, LANE), F32), pltpu.VMEM((1, RG_W), F32), pltpu.VMEM((tm + SUB, RG_W), F32)],
        compiler_params=_cparams(("arbitrary", "arbitrary")),
        name="prompt_in",
    )(x, mod, lw["g1"], lw["win"], lw["bs"], cs_tab, consts["pq"], consts["pfq"], consts["pk"], consts["pfk"],
      consts["cqk"], lw["rgcw"], lw["rgcb"], lw["wg"], lw["bg"], lw["lam"], lw["g_rg"], lw["gq"], lw["wuq"],
      lw["wuk"], lw["gkv"])


CHUNK_VREGS = 16


def _softmax_rows(s_sc, p_sc, a_sc, m_ref, l_ref, n_rows, tk, causal_diag):
    nlt = tk // LANE
    rc = max(2 * SUB, CHUNK_VREGS * SUB // nlt)
    for r in range(n_rows // rc):
        r0 = r * rc
        sl = slice(r0, r0 + rc)
        vis = min(nlt, (r0 + rc + LANE - 1) // LANE) if causal_diag else nlt
        tiles = []
        for j in range(vis):
            t = s_sc[sl, j * LANE:(j + 1) * LANE]
            if causal_diag and (j + 1) * LANE - 1 > r0:
                rr = lax.broadcasted_iota(jnp.int32, t.shape, 0) + r0
                cc = lax.broadcasted_iota(jnp.int32, t.shape, 1) + j * LANE
                t = jnp.where(cc <= rr, t, -jnp.inf)
            tiles.append(t)
        mx = tiles[0]
        for t in tiles[1:]:
            mx = jnp.maximum(mx, t)
        m_prev = m_ref[sl, :]
        m_new = jnp.maximum(m_prev, jnp.max(mx, axis=1, keepdims=True))
        alpha = jnp.exp(m_prev - m_new)
        psum = None
        for j, t in enumerate(tiles):
            pj = jnp.exp(t - m_new)
            p_sc[sl, j * LANE:(j + 1) * LANE] = pj.astype(BF16)
            psum = pj if psum is None else psum + pj
        for j in range(vis, nlt):
            p_sc[sl, j * LANE:(j + 1) * LANE] = jnp.zeros((rc, LANE), BF16)
        l_ref[sl, :] = alpha * l_ref[sl, :] + psum
        m_ref[sl, :] = m_new
        a_sc[sl, :] = alpha


def _pfox_body(qi_ref, ki_ref, qa_ref, ka_ref, v_ref, g_ref, o_ref, s_sc, p_sc, a_sc, m_sc, l_sc, acc_sc, *, tq):
    p = pl.program_id(1)
    qi = qi_ref[p]
    ki = ki_ref[p]

    @pl.when(ki == 0)
    def _():
        m_sc[...] = jnp.full_like(m_sc, -jnp.inf)
        l_sc[...] = jnp.zeros_like(l_sc)
        acc_sc[...] = jnp.zeros_like(acc_sc)

    def step(causal_diag):
        for hh in range(FOX_HEADS):
            q = qa_ref[0, :, hh * LANE:(hh + 1) * LANE]
            k = ka_ref[0, :, hh * LANE:(hh + 1) * LANE]
            s_sc[hh] = _dot_nt(q, k)
            _softmax_rows(s_sc.at[hh], p_sc.at[hh], a_sc.at[hh], m_sc.at[hh], l_sc.at[hh], tq, tq, causal_diag)
            half = hh // 2
            pv = _dot(p_sc[hh], v_ref[0, :, half * LANE:(half + 1) * LANE])
            acc_sc[hh] = a_sc[hh] * acc_sc[hh] + pv

    @pl.when(ki < qi)
    def _():
        step(False)

    @pl.when(ki == qi)
    def _():
        step(True)
        lane = lax.broadcasted_iota(jnp.int32, (tq, LANE), 1)
        halves = []
        for half in range(FOX_HEADS // 2):
            h0, h1 = 2 * half, 2 * half + 1
            inv0 = 1.0 / jnp.sum(l_sc[h0], axis=1, keepdims=True)
            inv1 = 1.0 / jnp.sum(l_sc[h1], axis=1, keepdims=True)
            halves.append(jnp.where(lane < FOX_DH, acc_sc[h0] * inv0, acc_sc[h1] * inv1))
        o = jnp.concatenate(halves, axis=1)
        o_ref[0] = (_rms(o) * g_ref[...]).astype(BF16)


def _tri_tables(n):
    qi = np.concatenate([np.full(i + 1, i, np.int32) for i in range(n)])
    ki = np.concatenate([np.arange(i + 1, dtype=np.int32) for i in range(n)])
    return jnp.asarray(qi), jnp.asarray(ki)


def _pfox_call(qa, ka, vb, g_fox, *, tq):
    b, t, _ = qa.shape
    nq = t // tq
    qi_tab, ki_tab = _tri_tables(nq)
    grid_spec = pltpu.PrefetchScalarGridSpec(
        num_scalar_prefetch=2,
        grid=(b, int(qi_tab.shape[0])),
        in_specs=[
            pl.BlockSpec((1, tq, 4 * LANE), lambda i, p, qi, ki: (i, qi[p], 0)),
            pl.BlockSpec((1, tq, 4 * LANE), lambda i, p, qi, ki: (i, ki[p], 0)),
            pl.BlockSpec((1, tq, FOX_W), lambda i, p, qi, ki: (i, ki[p], 0)),
            pl.BlockSpec((1, FOX_W), lambda i, p, qi, ki: (0, 0)),
        ],
        out_specs=pl.BlockSpec((1, tq, FOX_W), lambda i, p, qi, ki: (i, qi[p], 0)),
        scratch_shapes=[pltpu.VMEM((FOX_HEADS, tq, tq), F32), pltpu.VMEM((FOX_HEADS, tq, tq), BF16),
                        pltpu.VMEM((FOX_HEADS, tq, LANE), F32), pltpu.VMEM((FOX_HEADS, tq, LANE), F32),
                        pltpu.VMEM((FOX_HEADS, tq, LANE), F32), pltpu.VMEM((FOX_HEADS, tq, LANE), F32)],
    )
    return pl.pallas_call(
        functools.partial(_pfox_body, tq=tq),
        grid_spec=grid_spec,
        out_shape=jax.ShapeDtypeStruct((b, t, FOX_W), BF16),
        compiler_params=_cparams(("arbitrary", "arbitrary")),
        name="prompt_fox_attn",
    )(qi_tab, ki_tab, qa, ka, vb, g_fox)


def _pmla_body(qi_ref, ki_ref, q_ref, k_ref, wuv_ref, g_ref, o_ref, s_sc, p_sc, a_sc, m_sc, l_sc, acc_sc, *, tq):
    p = pl.program_id(1)
    qi = qi_ref[p]
    ki = ki_ref[p]

    @pl.when(ki == 0)
    def _():
        m_sc[...] = jnp.full_like(m_sc, -jnp.inf)
        l_sc[...] = jnp.zeros_like(l_sc)
        acc_sc[...] = jnp.zeros_like(acc_sc)

    def step(causal_diag):
        for hh in range(MLA_HEADS):
            s_sc[hh] = _dot_nt(q_ref[0, hh], k_ref[0])
            _softmax_rows(s_sc.at[hh], p_sc.at[hh], a_sc.at[hh], m_sc.at[hh], l_sc.at[hh], tq, tq, causal_diag)
            acc_sc[hh] = a_sc[hh] * acc_sc[hh] + _dot(p_sc[hh], k_ref[0, :, 0:KV_LORA])

    @pl.when(ki < qi)
    def _():
        step(False)

    @pl.when(ki == qi)
    def _():
        step(True)
        o = jnp.zeros((tq, MLA_W), F32)
        for hh in range(MLA_HEADS):
            o_lat = acc_sc[hh] * (1.0 / jnp.sum(l_sc[hh], axis=1, keepdims=True))
            o = o + _dot(o_lat.astype(BF16), wuv_ref[hh])
        o_ref[0] = (_rms(o) * g_ref[...]).astype(BF16)


def _pmla_call(qcat, kcat, wuv, g_mla, *, tq):
    b, _, t, _ = qcat.shape
    nq = t // tq
    qi_tab, ki_tab = _tri_tables(nq)
    grid_spec = pltpu.PrefetchScalarGridSpec(
        num_scalar_prefetch=2,
        grid=(b, int(qi_tab.shape[0])),
        in_specs=[
            pl.BlockSpec((1, MLA_HEADS, tq, 2 * LANE), lambda i, p, qi, ki: (i, 0, qi[p], 0)),
            pl.BlockSpec((1, tq, 2 * LANE), lambda i, p, qi, ki: (i, ki[p], 0)),
            pl.BlockSpec((MLA_HEADS, KV_LORA, MLA_W), lambda i, p, qi, ki: (0, 0, 0)),
            pl.BlockSpec((1, MLA_W), lambda i, p, qi, ki: (0, 0)),
        ],
        out_specs=pl.BlockSpec((1, tq, MLA_W), lambda i, p, qi, ki: (i, qi[p], 0)),
        scratch_shapes=[pltpu.VMEM((MLA_HEADS, tq, tq), F32), pltpu.VMEM((MLA_HEADS, tq, tq), BF16),
                        pltpu.VMEM((MLA_HEADS, tq, LANE), F32), pltpu.VMEM((MLA_HEADS, tq, LANE), F32),
                        pltpu.VMEM((MLA_HEADS, tq, LANE), F32), pltpu.VMEM((MLA_HEADS, tq, KV_LORA), F32)],
    )
    return pl.pallas_call(
        functools.partial(_pmla_body, tq=tq),
        grid_spec=grid_spec,
        out_shape=jax.ShapeDtypeStruct((b, t, MLA_W), BF16),
        compiler_params=_cparams(("arbitrary", "arbitrary")),
        name="prompt_mla_attn",
    )(qi_tab, ki_tab, qcat, kcat, wuv, g_mla)


def _ffn_conv_act(g, u, ext, pg, pu, cwg, cwu, cbg, cbu, n):
    def conv(cur, prev, cw, cb):
        ext[0:SUB, :] = prev
        ext[SUB:SUB + n, :] = cur
        y = cb + ext[SUB - 2:SUB - 2 + n, :] * cw[0:1, :]
        y = y + ext[SUB - 1:SUB - 1 + n, :] * cw[1:2, :]
        return y + cur * cw[2:3, :]
    gc = conv(g, pg, cwg, cbg)
    uc = conv(u, pu, cwu, cbu)
    return (_silu(gc) * uc).astype(BF16)


def _pffn_body(x_ref, fox_ref, rg_ref, mla_ref, wout_ref, mod_ref, g2_ref, wg_ref, wu_ref, cwg_ref, cwu_ref,
               cbg_ref, cbu_ref, wd_ref, gfin_ref,
               o_ref, stg_ref, stu_ref,
               x1_sc, h2_sc, acc_sc, ext_sc, carg_sc, caru_sc, *, tm, nc, final):
    t = pl.program_id(1)
    c = pl.program_id(2)
    mod = mod_ref[0, 0]

    @pl.when(c == 0)
    def _():
        mix = _dot(fox_ref[0], wout_ref[0:FOX_W, :])
        mix = mix + _dot(rg_ref[0], wout_ref[FOX_W:FOX_W + RG_W, :])
        mix = mix + _dot(mla_ref[0], wout_ref[FOX_W + RG_W:, :])
        x1 = x_ref[0] + mod[2:3, :] * mix
        x1_sc[...] = x1
        h2_sc[...] = (_rms(x1) * g2_ref[...] * (1.0 + mod[4:5, :]) + mod[3:4, :]).astype(BF16)
        acc_sc[...] = jnp.zeros_like(acc_sc)

    @pl.when(t == 0)
    def _():
        carg_sc[c] = jnp.zeros(carg_sc.shape[1:], F32)
        caru_sc[c] = jnp.zeros(caru_sc.shape[1:], F32)

    h2 = h2_sc[...]
    g = _dot(h2, wg_ref[...])
    u = _dot(h2, wu_ref[...])
    act = _ffn_conv_act(g, u, ext_sc, carg_sc[c], caru_sc[c], cwg_ref[...], cwu_ref[...], cbg_ref[...],
                        cbu_ref[...], tm)
    g_last = g[tm - SUB:tm, :]
    u_last = u[tm - SUB:tm, :]
    carg_sc[c] = g_last
    caru_sc[c] = u_last
    stg_ref[0, 0] = g_last
    stu_ref[0, 0] = u_last
    acc_sc[...] += _dot(act, wd_ref[...])

    @pl.when(c == nc - 1)
    def _():
        out = x1_sc[...] + mod[5:6, :] * acc_sc[...]
        if final:
            out = _rms(out) * gfin_ref[...]
        o_ref[0] = out


def _pffn_call(x, fox_n, rg_n, mla_n, mod, lw, gfin, *, tm, final):
    b, t, d = x.shape
    dff = lw["wd"].shape[0]
    nc = 2
    fc = dff // nc
    nt = t // tm
    full = lambda shape: pl.BlockSpec(shape, lambda i, j, c: (0,) * len(shape))
    row = lambda w: pl.BlockSpec((1, tm, w), lambda i, j, c: (i, j, 0))
    in_specs = [
        row(d), row(FOX_W), row(RG_W), row(MLA_W), full((d, d)),
        pl.BlockSpec((1, 1, 6, d), lambda i, j, c: (i, 0, 0, 0)), full((1, d)),
        pl.BlockSpec((d, fc), lambda i, j, c: (0, c)),
        pl.BlockSpec((d, fc), lambda i, j, c: (0, nc + c)),
        pl.BlockSpec((FFN_CONV, fc), lambda i, j, c: (0, c)),
        pl.BlockSpec((FFN_CONV, fc), lambda i, j, c: (0, nc + c)),
        pl.BlockSpec((1, fc), lambda i, j, c: (0, c)),
        pl.BlockSpec((1, fc), lambda i, j, c: (0, nc + c)),
        pl.BlockSpec((fc, d), lambda i, j, c: (c, 0)),
        full((1, d)),
    ]
    out_specs = [
        row(d),
        pl.BlockSpec((1, 1, SUB, fc), lambda i, j, c: (i, j, 0, c)),
        pl.BlockSpec((1, 1, SUB, fc), lambda i, j, c: (i, j, 0, c)),
    ]
    out_shape = [jax.ShapeDtypeStruct((b, t, d), F32), jax.ShapeDtypeStruct((b, nt, SUB, dff), F32),
                 jax.ShapeDtypeStruct((b, nt, SUB, dff), F32)]
    return pl.pallas_call(
        functools.partial(_pffn_body, tm=tm, nc=nc, final=final),
        grid=(b, nt, nc),
        in_specs=in_specs,
        out_specs=out_specs,
        out_shape=out_shape,
        scratch_shapes=[pltpu.VMEM((tm, d), F32), pltpu.VMEM((tm, d), BF16), pltpu.VMEM((tm, d), F32),
                        pltpu.VMEM((tm + SUB, fc), F32), pltpu.VMEM((nc, SUB, fc), F32),
                        pltpu.VMEM((nc, SUB, fc), F32)],
        compiler_params=_cparams(("arbitrary", "arbitrary", "arbitrary")),
        name="prompt_out_ffn",
    )(x, fox_n, rg_n, mla_n, lw["wout"], mod, lw["g2"], lw["wup"], lw["wup"], lw["fcw"], lw["fcw"],
      lw["fcb"], lw["fcb"], lw["wd"], gfin)


def _sin_body(x_ref, mod_ref, g1_ref, win_ref, bs_ref, cs_ref, rgcw_ref, rgcb_ref, wg_ref, bg_ref, lam_ref,
              grg_ref, gq_ref, wuq_ref, wuk_ref, gkv_ref, h0_ref, cprev_ref,
              fq_ref, fk_ref, fv_ref, lfl_ref, lf_ref, qlat_ref, qtail_ref, ckv_ref, tail_ref, kr_ref,
              rgn_ref, rgh_ref, zx_ref):
    d = x_ref.shape[1]
    x = x_ref[...]
    sh_a = mod_ref[:, 0:d]
    sc_a = mod_ref[:, d:2 * d]
    h = _rms(x) * g1_ref[...] * (1.0 + sc_a) + sh_a
    hb = h.astype(BF16)

    fq_ref[...] = _dot(hb, win_ref[:, Z_Q:Z_Q + FOX_W]) * FOX_SCALE
    fk_ref[...] = _dot(hb, win_ref[:, Z_K:Z_K + FOX_W])
    fv_ref[...] = _dot(hb, win_ref[:, Z_V:Z_V + FOX_W])
    zs = _dot(hb, win_ref[:, Z_S:Z_S + LANE])
    lane = lax.broadcasted_iota(jnp.int32, zs.shape, 1)
    fmask = (lane >= S_F) & (lane < S_F + FOX_HEADS)
    lf = jnp.where(fmask, _log_sigmoid(zs + bs_ref[...]), 0.0)
    lfl_ref[...] = lf
    lf_ref[...] = pltpu.roll(lf, LANE - S_F, 1)[:, :FOX_HEADS]

    cs = jnp.broadcast_to(cs_ref[...], zs.shape)
    q_lat, q_rope = _mla_queries(_dot(hb, win_ref[:, Z_CQ:Z_CQ + Q_LORA_PAD]), gq_ref[...], wuq_ref[...],
                                 wuk_ref[...], cs)
    qlat_ref[...] = q_lat
    qtail_ref[...] = q_rope
    ckv = _rms(_dot(hb, win_ref[:, Z_CKV:Z_CKV + KV_LORA])) * gkv_ref[...]
    ckv_ref[...] = ckv
    kr = _rope_small(zs, cs)
    tail_ref[...] = kr
    kr_ref[...] = kr[:, :MLA_ROPE]

    zy = _dot(hb, win_ref[:, Z_Y:Z_Y + RG_W])
    zx = _dot(hb, win_ref[:, Z_X:Z_X + RG_W])
    zx_ref[...] = zx
    cw = rgcw_ref[...]
    xc = rgcb_ref[...] + cprev_ref[0] * cw[0:1, :]
    xc = xc + cprev_ref[1] * cw[1:2, :]
    xc = xc + cprev_ref[2] * cw[2:3, :]
    xc = xc + zx * cw[3:4, :]
    a, u = _rg_gates(xc, wg_ref[...], bg_ref[...], lam_ref[...])
    hn = a * h0_ref[...] + u
    rgh_ref[...] = hn
    o = hn * _gelu_tanh(zy)
    rgn_ref[...] = (_rms(o) * grg_ref[...]).astype(BF16)


def _sin_call(x, mod, lw, cs_row, h0, cprev):
    n, d = x.shape
    sds = lambda w, dt=F32: jax.ShapeDtypeStruct((n, w), dt)
    out_shape = [sds(FOX_W), sds(FOX_W), sds(FOX_W), sds(LANE), sds(FOX_HEADS), sds(4 * LANE), sds(4 * LANE),
                 sds(KV_LORA), sds(LANE), sds(MLA_ROPE), sds(RG_W, BF16), sds(RG_W), sds(RG_W)]
    return pl.pallas_call(
        _sin_body,
        out_shape=out_shape,
        compiler_params=pltpu.CompilerParams(vmem_limit_bytes=VMEM_LIMIT),
        name="sample_in",
    )(x, mod, lw["g1"], lw["win"], lw["bs"], cs_row, lw["rgcw"], lw["rgcb"], lw["wg"], lw["bg"], lw["lam"],
      lw["g_rg"], lw["gq"], lw["wuq"], lw["wuk"], lw["gkv"], h0, cprev)


def _rows_from_lanes(row_vec, width, n_heads):
    r = lax.broadcasted_iota(jnp.int32, (SUB, width), 0)
    out = jnp.zeros((SUB, width), F32)
    for hh in range(n_heads):
        piece = jnp.broadcast_to(row_vec[:, hh * width:(hh + 1) * width], (SUB, width))
        out = jnp.where(r == hh, piece, out)
    return out


def _sattn_body(pt_ref, fq_ref, fkn_ref, fvn_ref, lfn_ref, qlat_ref, qtail_ref, ckvn_ref, tailn_ref,
                kc_ref, vc_ref, lc_ref, cc_ref, rc_ref,
                ofox_ref, olat_ref,
                kbuf, vbuf, lbuf, cbuf, rbuf, sems, mf_sc, lf_sc, af_sc, mm_sc, lm_sc, am_sc, tail_sc,
                *, pc, nch, nb, layer):
    b = pl.program_id(0)
    c = pl.program_id(1)
    step = b * nch + c
    slot = step % 2
    total = nb * nch

    def copies(bb, cc, sl, j):
        pg = pt_ref[bb, (nch - 1 - cc) * pc + j]
        return (
            pltpu.make_async_copy(kc_ref.at[layer, pg], kbuf.at[sl, j], sems.at[0, sl]),
            pltpu.make_async_copy(vc_ref.at[layer, pg], vbuf.at[sl, j], sems.at[1, sl]),
            pltpu.make_async_copy(lc_ref.at[layer, pg], lbuf.at[sl, j], sems.at[2, sl]),
            pltpu.make_async_copy(cc_ref.at[layer, pg], cbuf.at[sl, j], sems.at[3, sl]),
            pltpu.make_async_copy(rc_ref.at[layer, pg], rbuf.at[sl, j], sems.at[4, sl]),
        )

    def start_chunk(bb, cc, sl):
        def body(j, carry):
            for cp in copies(bb, cc, sl, j):
                cp.start()
            return carry
        lax.fori_loop(0, pc, body, 0)

    def wait_chunk(bb, cc, sl):
        def body(j, carry):
            for cp in copies(bb, cc, sl, j):
                cp.wait()
            return carry
        lax.fori_loop(0, pc, body, 0)

    @pl.when(step == 0)
    def _():
        start_chunk(b, c, slot)

    @pl.when(step + 1 < total)
    def _():
        nxt = step + 1
        start_chunk(nxt // nch, nxt % nch, 1 - slot)

    lane256 = lax.broadcasted_iota(jnp.int32, (SUB, FOX_W), 1)
    row256 = lax.broadcasted_iota(jnp.int32, (SUB, FOX_W), 0)
    headsel = (lane256 >= row256 * FOX_DH) & (lane256 < (row256 + 1) * FOX_DH)
    qf = jnp.where(headsel, jnp.broadcast_to(fq_ref[0], (SUB, FOX_W)), 0.0)
    ql = _rows_from_lanes(qlat_ref[0], LANE, MLA_HEADS)
    qt = _rows_from_lanes(qtail_ref[0], LANE, MLA_HEADS)[:, :MLA_ROPE]
    lane128 = lax.broadcasted_iota(jnp.int32, (SUB, LANE), 1)
    row128 = lax.broadcasted_iota(jnp.int32, (SUB, LANE), 0)
    lfn = jnp.broadcast_to(lfn_ref[0], (SUB, LANE))
    lf_new = jnp.sum(jnp.where(lane128 == row128 + S_F, lfn, 0.0), axis=1, keepdims=True)

    @pl.when(c == 0)
    def _():
        mf_sc[...] = jnp.full_like(mf_sc, -jnp.inf)
        lf_sc[...] = jnp.zeros_like(lf_sc)
        af_sc[...] = jnp.zeros_like(af_sc)
        mm_sc[...] = jnp.full_like(mm_sc, -jnp.inf)
        lm_sc[...] = jnp.zeros_like(lm_sc)
        am_sc[...] = jnp.zeros_like(am_sc)
        tail_sc[...] = lf_new

    wait_chunk(b, c, slot)

    qf_b = qf.astype(BF16)
    ql_b = ql.astype(BF16)
    qt_b = qt.astype(BF16)

    s_f, s_m, tot = [], [], []
    zero4 = jnp.zeros((SUB - FOX_HEADS, LANE), F32)
    for j in range(pc):
        s_f.append(_dot(qf_b, kbuf[slot, j].astype(BF16)))
        sm = _dot_nt(ql_b, cbuf[slot, j].astype(BF16)) + _dot(qt_b, rbuf[slot, j].astype(BF16))
        s_m.append(sm)
        x = jnp.concatenate([lbuf[slot, j], zero4], axis=0)
        incl = x
        sh = 1
        while sh < LANE:
            incl = incl + jnp.where(lane128 + sh < LANE, pltpu.roll(incl, LANE - sh, 1), 0.0)
            sh *= 2
        s_f[j] = s_f[j] + (incl - x)
        tot.append(incl[:, 0:1])
    run = tail_sc[...]
    for j in range(pc - 1, -1, -1):
        s_f[j] = s_f[j] + run
        run = run + tot[j]
    tail_sc[...] = run

    def online(s_list, m_sc, l_sc):
        m_prev = m_sc[...]
        mx = s_list[0]
        for s in s_list[1:]:
            mx = jnp.maximum(mx, s)
        m_new = jnp.maximum(m_prev, jnp.max(mx, axis=1, keepdims=True))
        alpha = jnp.exp(m_prev - m_new)
        ps = [jnp.exp(s - m_new) for s in s_list]
        sm = ps[0]
        for p_ in ps[1:]:
            sm = sm + p_
        l_sc[...] = alpha * l_sc[...] + jnp.sum(sm, axis=1, keepdims=True)
        m_sc[...] = m_new
        return alpha, ps

    alpha_f, p_f = online(s_f, mf_sc, lf_sc)
    acc = jnp.zeros((SUB, FOX_W), F32)
    for j in range(pc):
        acc = acc + _dot_nt(p_f[j].astype(BF16), vbuf[slot, j].astype(BF16))
    af_sc[...] = alpha_f * af_sc[...] + acc

    alpha_m, p_m = online(s_m, mm_sc, lm_sc)
    accm = jnp.zeros((SUB, KV_LORA), F32)
    for j in range(pc):
        accm = accm + _dot(p_m[j].astype(BF16), cbuf[slot, j].astype(BF16))
    am_sc[...] = alpha_m * am_sc[...] + accm

    @pl.when(c == nch - 1)
    def _():
        kn = jnp.broadcast_to(fkn_ref[0], (SUB, FOX_W))
        vn = jnp.broadcast_to(fvn_ref[0], (SUB, FOX_W))
        s_new = jnp.sum(qf * kn, axis=1, keepdims=True)
        m_prev = mf_sc[...]
        m_new = jnp.maximum(m_prev, s_new)
        al = jnp.exp(m_prev - m_new)
        pn = jnp.exp(s_new - m_new)
        l_fin = al * lf_sc[...] + pn
        o8 = (al * af_sc[...] + pn * vn) * (1.0 / l_fin)
        ofox_ref[0] = jnp.sum(jnp.where(headsel, o8, 0.0), axis=0, keepdims=True)

        cn = jnp.broadcast_to(ckvn_ref[0], (SUB, KV_LORA))
        tn = jnp.broadcast_to(tailn_ref[0], (SUB, LANE))
        qt_full = _rows_from_lanes(qtail_ref[0], LANE, MLA_HEADS)
        s_new = jnp.sum(ql * cn, axis=1, keepdims=True) + jnp.sum(qt_full * tn, axis=1, keepdims=True)
        m_prev = mm_sc[...]
        m_new = jnp.maximum(m_prev, s_new)
        al = jnp.exp(m_prev - m_new)
        pn = jnp.exp(s_new - m_new)
        l_fin = al * lm_sc[...] + pn
        olat_ref[0] = (al * am_sc[...] + pn * cn) * (1.0 / l_fin)


def _sattn_call(page_table, fq, fkn, fvn, lfn, qlat, qtail, ckvn, tailn, kc, vc, lc, cc, rc, *, nch, layer):
    nb, npg = page_table.shape
    pc = npg // nch
    page = kc.shape[-1]
    r3 = lambda a: a.reshape(nb, 1, a.shape[-1])
    vec = lambda w: pl.BlockSpec((1, 1, w), lambda i, j, pt: (i, 0, 0))
    any_spec = pl.BlockSpec(memory_space=pl.ANY)
    grid_spec = pltpu.PrefetchScalarGridSpec(
        num_scalar_prefetch=1,
        grid=(nb, nch),
        in_specs=[vec(FOX_W), vec(FOX_W), vec(FOX_W), vec(LANE), vec(4 * LANE), vec(4 * LANE), vec(KV_LORA),
                  vec(LANE), any_spec, any_spec, any_spec, any_spec, any_spec],
        out_specs=[pl.BlockSpec((1, 1, FOX_W), lambda i, j, pt: (i, 0, 0)),
                   pl.BlockSpec((1, SUB, KV_LORA), lambda i, j, pt: (i, 0, 0))],
        scratch_shapes=[
            pltpu.VMEM((2, pc, FOX_W, page), F32), pltpu.VMEM((2, pc, FOX_W, page), F32),
            pltpu.VMEM((2, pc, FOX_HEADS, page), F32), pltpu.VMEM((2, pc, page, KV_LORA), F32),
            pltpu.VMEM((2, pc, MLA_ROPE, page), F32), pltpu.SemaphoreType.DMA((5, 2)),
            pltpu.VMEM((SUB, 1), F32), pltpu.VMEM((SUB, 1), F32), pltpu.VMEM((SUB, FOX_W), F32),
            pltpu.VMEM((SUB, 1), F32), pltpu.VMEM((SUB, 1), F32), pltpu.VMEM((SUB, KV_LORA), F32),
            pltpu.VMEM((SUB, 1), F32),
        ],
    )
    return pl.pallas_call(
        functools.partial(_sattn_body, pc=pc, nch=nch, nb=nb, layer=layer),
        grid_spec=grid_spec,
        out_shape=[jax.ShapeDtypeStruct((nb, 1, FOX_W), F32), jax.ShapeDtypeStruct((nb, SUB, KV_LORA), F32)],
        compiler_params=_cparams(("arbitrary", "arbitrary")),
        name="sample_paged_attn",
    )(page_table, r3(fq), r3(fkn), r3(fvn), r3(lfn), r3(qlat), r3(qtail), r3(ckvn), r3(tailn), kc, vc, lc, cc, rc)


def _sffn_body(x_ref, ofox_ref, rg_ref, olat_ref, wuv_ref, gfox_ref, gmla_ref, wout_ref, mod_ref, g2_ref,
               wg_ref, wu_ref, cwg_ref, cwu_ref, cbg_ref, cbu_ref, p0g_ref, p0u_ref, p1g_ref, p1u_ref, wd_ref,
               gfin_ref,
               o_ref, upg_ref, upu_ref,
               x1_sc, h2_sc, acc_sc, *, nc, final):
    c = pl.program_id(0)
    d = x_ref.shape[1]

    @pl.when(c == 0)
    def _():
        fox_n = (_rms(ofox_ref[...]) * gfox_ref[...]).astype(BF16)
        o_mla = _dot(olat_ref[:, 0:MLA_HEADS * KV_LORA].astype(BF16), wuv_ref[...])
        mla_n = (_rms(o_mla) * gmla_ref[...]).astype(BF16)
        mix = _dot(fox_n, wout_ref[0:FOX_W, :])
        mix = mix + _dot(rg_ref[...], wout_ref[FOX_W:FOX_W + RG_W, :])
        mix = mix + _dot(mla_n, wout_ref[FOX_W + RG_W:, :])
        x1 = x_ref[...] + mod_ref[:, 2 * d:3 * d] * mix
        x1_sc[...] = x1
        h2_sc[...] = (_rms(x1) * g2_ref[...] * (1.0 + mod_ref[:, 4 * d:5 * d]) + mod_ref[:, 3 * d:4 * d]).astype(BF16)
        acc_sc[...] = jnp.zeros_like(acc_sc)

    h2 = h2_sc[...]
    g = _dot(h2, wg_ref[...])
    u = _dot(h2, wu_ref[...])
    upg_ref[...] = g
    upu_ref[...] = u
    cwg = cwg_ref[...]
    cwu = cwu_ref[...]
    gc = cbg_ref[...] + p0g_ref[...] * cwg[0:1, :] + p1g_ref[...] * cwg[1:2, :] + g * cwg[2:3, :]
    uc = cbu_ref[...] + p0u_ref[...] * cwu[0:1, :] + p1u_ref[...] * cwu[1:2, :] + u * cwu[2:3, :]
    act = (_silu(gc) * uc).astype(BF16)
    acc_sc[...] += _dot(act, wd_ref[...])

    @pl.when(c == nc - 1)
    def _():
        out = x1_sc[...] + mod_ref[:, 5 * d:6 * d] * acc_sc[...]
        if final:
            out = _rms(out) * gfin_ref[...]
        o_ref[...] = out


def _sffn_call(x, ofox, rg_n, olat, mod, lw, prev, gfin, *, final):
    n, d = x.shape
    dff = lw["wd"].shape[0]
    nc = 2
    fc = dff // nc
    full = lambda shape: pl.BlockSpec(shape, lambda c: (0,) * len(shape))
    col = lambda rows, off: pl.BlockSpec((rows, fc), lambda c, off=off: (0, off + c))
    in_specs = [
        full((n, d)), full((n, FOX_W)), full((n, RG_W)), full((n, SUB * KV_LORA)),
        full((MLA_HEADS * KV_LORA, MLA_W)), full((1, FOX_W)), full((1, MLA_W)), full((d, d)), full((n, 6 * d)),
        full((1, d)),
        col(d, 0), col(d, nc), col(FFN_CONV, 0), col(FFN_CONV, nc), col(1, 0), col(1, nc),
        col(n, 0), col(n, nc), col(n, 2 * nc), col(n, 3 * nc),
        pl.BlockSpec((fc, d), lambda c: (c, 0)),
        full((1, d)),
    ]
    out_specs = [full((n, d)), col(n, 0), col(n, 0)]
    out_shape = [jax.ShapeDtypeStruct((n, d), F32), jax.ShapeDtypeStruct((n, dff), F32),
                 jax.ShapeDtypeStruct((n, dff), F32)]
    return pl.pallas_call(
        functools.partial(_sffn_body, nc=nc, final=final),
        grid=(nc,),
        in_specs=in_specs,
        out_specs=out_specs,
        out_shape=out_shape,
        scratch_shapes=[pltpu.VMEM((n, d), F32), pltpu.VMEM((n, d), BF16), pltpu.VMEM((n, d), F32)],
        compiler_params=_cparams(("arbitrary",)),
        name="sample_out_ffn",
    )(x, ofox, rg_n, olat, lw["wuv_bd"], lw["g_fox"], lw["g_mla"], lw["wout"], mod, lw["g2"], lw["wup"], lw["wup"],
      lw["fcw"], lw["fcw"], lw["fcb"], lw["fcb"], prev, prev, prev, prev, lw["wd"], gfin)


def _swap_half(w):
    half = w.shape[-1] // 2
    return jnp.concatenate([-w[..., half:], w[..., :half]], axis=-1)


def _prep_weights(norm1_g, norm2_g, w_in, fox_bf, rg_conv_w, rg_conv_b, rg_wa, rg_ba, rg_wx, rg_bx, rg_lam,
                  mla_q_g, w_uq, mla_kv_g, w_uk, w_uv, mix_g, w_out, w_up, ffn_conv_w, ffn_conv_b, w_down):
    depth, d, _ = w_in.shape
    o = np.cumsum((0, FOX_W, FOX_W, FOX_W, FOX_HEADS, RG_W, RG_W, Q_LORA, KV_LORA, MLA_ROPE))
    seg = lambda i: w_in[:, :, o[i]:o[i + 1]]
    wq, wk, wv, wf, wy, wx, wcq, wckv, wkr = [seg(i) for i in range(9)]
    z = lambda n: jnp.zeros((depth, d, n), w_in.dtype)
    win = jnp.concatenate([wq, wk, wv, wy, wx, wcq, z(Q_LORA_PAD - Q_LORA), wckv, wkr, _swap_half(wkr), wf,
                           z(LANE - 2 * MLA_ROPE - FOX_HEADS)], axis=-1).astype(BF16)
    bs = jnp.zeros((depth, 1, LANE), F32).at[:, 0, S_F:S_F + FOX_HEADS].set(fox_bf)

    eye_rg = jnp.eye(RG_HEADS, dtype=F32)
    bd = lambda w: jnp.einsum("lhij,hg->lhigj", w, eye_rg).reshape(depth, RG_W, RG_W)
    wg = jnp.concatenate([bd(rg_wa), bd(rg_wx)], axis=-1).astype(BF16)
    bg = jnp.concatenate([rg_ba, rg_bx], axis=-1).reshape(depth, 1, 2 * RG_W)

    nope = w_uq[..., :MLA_NOPE].reshape(depth, Q_LORA, MLA_HEADS * MLA_NOPE)
    rope = w_uq[..., MLA_NOPE:]
    rope_pad = jnp.concatenate([rope, _swap_half(rope),
                                jnp.zeros((depth, Q_LORA, MLA_HEADS, LANE - 2 * MLA_ROPE), F32)], axis=-1)
    wuq = jnp.concatenate([nope, rope_pad.reshape(depth, Q_LORA, MLA_HEADS * LANE)], axis=-1)
    wuq = jnp.pad(wuq, ((0, 0), (0, Q_LORA_PAD - Q_LORA), (0, 0))).astype(BF16)
    eye_h = jnp.eye(MLA_HEADS, dtype=F32)
    wuk = jnp.einsum("lchn,hg->lhngc", w_uk, eye_h).reshape(depth, MLA_HEADS * MLA_NOPE, MLA_HEADS * KV_LORA)
    wuv = jnp.einsum("lchd,hg->lhcgd", w_uv, eye_h).reshape(depth, MLA_HEADS, KV_LORA, MLA_W)
    gq = jnp.pad(mla_q_g, ((0, 0), (0, Q_LORA_PAD - Q_LORA))).reshape(depth, 1, Q_LORA_PAD)

    r2 = lambda a: a.reshape(depth, 1, a.shape[-1])
    return dict(
        g1=r2(norm1_g), g2=r2(norm2_g), win=win, bs=bs,
        rgcw=rg_conv_w, rgcb=r2(rg_conv_b), wg=wg, bg=bg, lam=r2(rg_lam),
        gq=gq, wuq=wuq, wuk=wuk.astype(BF16), gkv=r2(mla_kv_g),
        wuv=wuv.astype(BF16), wuv_bd=wuv.reshape(depth, MLA_HEADS * KV_LORA, MLA_W).astype(BF16),
        g_fox=r2(mix_g[:, :FOX_W]), g_rg=r2(mix_g[:, FOX_W:FOX_W + RG_W]), g_mla=r2(mix_g[:, FOX_W + RG_W:]),
        wout=w_out.astype(BF16), wup=w_up.astype(BF16), fcw=ffn_conv_w, fcb=r2(ffn_conv_b), wd=w_down.astype(BF16),
    )


def _fox_placement():
    pq = np.zeros((FOX_W, 4 * LANE), np.float32)
    pk = np.zeros((FOX_W, 4 * LANE), np.float32)
    pfq = np.zeros((LANE, 4 * LANE), np.float32)
    pfk = np.zeros((LANE, 4 * LANE), np.float32)
    cqk = np.zeros((2, 4 * LANE), np.float32)
    for h in range(FOX_HEADS):
        for dd in range(FOX_DH):
            pq[h * FOX_DH + dd, h * LANE + dd] = FOX_SCALE
            pk[h * FOX_DH + dd, h * LANE + dd] = 1.0
        for piece in range(3):
            src = S_F + piece * FOX_HEADS + h
            pfq[src, h * LANE + A_F + piece] = 1.0
            cqk[0, h * LANE + A_F + 3 + piece] = 1.0
            cqk[1, h * LANE + A_F + piece] = 1.0
            pfk[src, h * LANE + A_F + 3 + piece] = -1.0
    return dict(pq=jnp.asarray(pq, BF16), pk=jnp.asarray(pk, BF16), pfq=jnp.asarray(pfq, BF16),
                pfk=jnp.asarray(pfk, BF16), cqk=jnp.asarray(cqk, F32))


def _rope_table(pos):
    half = MLA_ROPE // 2
    inv = ROPE_BASE ** (-jnp.arange(half, dtype=F32) / half)
    ang = pos.astype(F32)[:, None] * inv[None, :]
    cos, sin = jnp.cos(ang), jnp.sin(ang)
    return jnp.concatenate([cos, cos, sin, sin, jnp.zeros((pos.shape[0], LANE - 2 * MLA_ROPE), F32)], axis=-1)


def _forward(x_prompt, x_sample, c_prompt, c_sample, cache_fox_k, cache_fox_v, cache_fox_logf, cache_mla_ckv,
             cache_mla_krope, state_rglru_h, state_rglru_conv, state_ffn_conv, page_table, norm1_g, norm2_g, w_ada,
             b_ada, w_in, fox_bf, rg_conv_w, rg_conv_b, rg_wa, rg_ba, rg_wx, rg_bx, rg_lam, mla_q_g, w_uq, mla_kv_g,
             w_uk, w_uv, mix_g, w_out, w_up, ffn_conv_w, ffn_conv_b, w_down, final_g, *, tm, tq, nch):
    depth = w_in.shape[0]
    bp, t, d = x_prompt.shape
    bs = x_sample.shape[0]
    npool, page = cache_fox_k.shape[1], cache_fox_k.shape[2]
    past_len = page_table.shape[1] * page
    dff = w_down.shape[1]

    wts = _prep_weights(norm1_g, norm2_g, w_in, fox_bf, rg_conv_w, rg_conv_b, rg_wa, rg_ba, rg_wx, rg_bx, rg_lam,
                        mla_q_g, w_uq, mla_kv_g, w_uk, w_uv, mix_g, w_out, w_up, ffn_conv_w, ffn_conv_b, w_down)
    consts = _fox_placement()
    gfin = final_g.reshape(1, d)

    mp = -(-(bp + bs) // SUB) * SUB
    c_all = jnp.concatenate([c_prompt, c_sample, jnp.zeros((mp - bp - bs, d), F32)], axis=0)
    mod_all = _mod_call(c_all, w_ada, b_ada)
    mod_p = mod_all[:, :bp].reshape(depth, bp, 1, 6, d)
    mod_s = mod_all[:, bp:bp + bs]

    cs_p = _rope_table(jnp.arange(t, dtype=jnp.int32))
    cs_s = _rope_table(jnp.full((1,), past_len, jnp.int32))

    kc = jnp.transpose(cache_fox_k, (0, 1, 3, 4, 2)).reshape(depth, npool, FOX_W, page)
    vc = jnp.transpose(cache_fox_v, (0, 1, 3, 4, 2)).reshape(depth, npool, FOX_W, page)
    lc = jnp.transpose(cache_fox_logf, (0, 1, 3, 2))
    rc = jnp.transpose(cache_mla_krope, (0, 1, 3, 2))
    conv_prev = jnp.transpose(state_rglru_conv, (0, 2, 1, 3))
    ffn_prev = state_ffn_conv.reshape(depth, bs, (FFN_CONV - 1) * 2 * dff)

    outs_p = [[] for _ in range(8)]
    outs_s = [[] for _ in range(8)]
    xp = x_prompt
    xs = x_sample.reshape(bs, d)
    for l in range(depth):
        lw = {k: v[l] for k, v in wts.items()}
        final = l == depth - 1

        (fk, fv, lf, qa, ka, vb, ckv, kr, kcat, qcat, rgn, rgh8, rgc8) = _pin_call(xp, mod_p[l], lw, consts, cs_p, tm=tm)
        fox_n = _pfox_call(qa, ka, vb, lw["g_fox"], tq=tq)
        mla_n = _pmla_call(qcat, kcat, lw["wuv"], lw["g_mla"], tq=tq)
        xp, stg, stu = _pffn_call(xp, fox_n, rgn, mla_n, mod_p[l], lw, gfin, tm=tm, final=final)
        ffc = jnp.concatenate([stg[:, -1], stu[:, -1]], axis=-1)[:, SUB - (FFN_CONV - 1):]
        for lst, val in zip(outs_p, (fk.reshape(bp, t, FOX_HEADS, FOX_DH), fv.reshape(bp, t, FOX_HEADS, FOX_DH), lf,
                                     ckv, kr, rgh8[:, SUB - 1], rgc8[:, SUB - (RG_CONV - 1):], ffc)):
            lst.append(val)

        (fq, fkn, fvn, lfl, lfn, qlat, qtail, ckvn, tailn, krn, rgn_s, rgh_s, zx_s) = _sin_call(
            xs, mod_s[l], lw, cs_s, state_rglru_h[l], conv_prev[l])
        ofox, olat = _sattn_call(page_table, fq, fkn, fvn, lfl, qlat, qtail, ckvn, tailn,
                                 kc, vc, lc, cache_mla_ckv, rc, nch=nch, layer=l)
        xs, upg, upu = _sffn_call(xs, ofox.reshape(bs, FOX_W), rgn_s, olat.reshape(bs, SUB * KV_LORA), mod_s[l], lw,
                                  ffn_prev[l], gfin, final=final)
        rgc_new = jnp.concatenate([state_rglru_conv[l][:, 1:], zx_s[:, None, :]], axis=1)
        ffc_new = jnp.stack([state_ffn_conv[l][:, 1], jnp.concatenate([upg, upu], axis=-1)], axis=1)
        for lst, val in zip(outs_s, (fkn.reshape(bs, 1, FOX_HEADS, FOX_DH), fvn.reshape(bs, 1, FOX_HEADS, FOX_DH),
                                     lfn.reshape(bs, 1, FOX_HEADS), ckvn.reshape(bs, 1, KV_LORA),
                                     krn.reshape(bs, 1, MLA_ROPE), rgh_s, rgc_new, ffc_new)):
            lst.append(val)

    sp = [jnp.stack(v) for v in outs_p]
    ss = [jnp.stack(v) for v in outs_s]
    res = [xp, xs.reshape(bs, 1, d)]
    for a, b_ in zip(sp, ss):
        res += [a, b_]
    return tuple(res)


def kernel(x_prompt, x_sample, c_prompt, c_sample, cache_fox_k, cache_fox_v, cache_fox_logf, cache_mla_ckv, cache_mla_krope, state_rglru_h, state_rglru_conv, state_ffn_conv, page_table, norm1_g, norm2_g, w_ada, b_ada, w_in, fox_bf, rg_conv_w, rg_conv_b, rg_wa, rg_ba, rg_wx, rg_bx, rg_lam, mla_q_g, w_uq, mla_kv_g, w_uk, w_uv, mix_g, w_out, w_up, ffn_conv_w, ffn_conv_b, w_down, final_g):
    t = x_prompt.shape[1]
    npg = page_table.shape[1]
    tm = min(512, t)
    tq = min(1024, t)
    nch = 2 if npg % 2 == 0 and npg >= 2 else 1
    return _forward(x_prompt, x_sample, c_prompt, c_sample, cache_fox_k, cache_fox_v, cache_fox_logf, cache_mla_ckv,
                    cache_mla_krope, state_rglru_h, state_rglru_conv, state_ffn_conv, page_table, norm1_g, norm2_g,
                    w_ada, b_ada, w_in, fox_bf, rg_conv_w, rg_conv_b, rg_wa, rg_ba, rg_wx, rg_bx, rg_lam, mla_q_g,
                    w_uq, mla_kv_g, w_uk, w_uv, mix_g, w_out, w_up, ffn_conv_w, ffn_conv_b, w_down, final_g,
                    tm=tm, tq=tq, nch=nch)
```
